```python
import math
import jax, jax.numpy as jnp
from jax import lax
import numpy as np

D_MODEL = 2048
BATCH = 4
SEQ = 4096
DEPTH = 1

DA_HEADS = 8
DA_HEAD_DIM = 64
DA_V_DIM = 2 * DA_HEAD_DIM
DA_QK_WIDTH = DA_HEADS * 2 * DA_HEAD_DIM
DA_V_WIDTH = DA_HEADS * DA_V_DIM
ROPE_THETA = 500000.0
ROPE_DIM = DA_HEAD_DIM // 4
Q_BLOCK = 128
RW_HEADS = 16
RW_HEAD_DIM = 64
RW_WIDTH = RW_HEADS * RW_HEAD_DIM
DECAY_LORA = 64
AAA_LORA = 64
GATE_LORA = 160
GN_EPS = 64e-5
N_BRANCH = 2
N_EXPERTS = 16
EXPERT_FF = 1024
CAPACITY_FACTOR = 2
RMS_EPS = 1e-6

DA_COLS = 2 * DA_QK_WIDTH + DA_V_WIDTH
RW_COLS = 3 * RW_WIDTH + 2 * DECAY_LORA + 2 * AAA_LORA + GATE_LORA
GATE_COLS = N_BRANCH * D_MODEL
IN_COLS = DA_COLS + RW_COLS + GATE_COLS
RW_SPLITS = tuple(int(v) for v in np.cumsum(
    [RW_WIDTH, RW_WIDTH, RW_WIDTH, DECAY_LORA, DECAY_LORA, AAA_LORA, AAA_LORA]))

kernel_name = "hybrid_diffattn_rwkv7_ecmoe_encoder"


def rms_norm(x, g, eps=RMS_EPS):
    xf = x.astype(jnp.float32)
    y = xf * lax.rsqrt(jnp.mean(xf * xf, axis=-1, keepdims=True) + eps)
    return (y * g.astype(jnp.float32)).astype(x.dtype)


def rope_tables(T):
    inv = ROPE_THETA ** (-(jnp.arange(0, ROPE_DIM, 2, dtype=jnp.float32) / ROPE_DIM))
    ang = jnp.arange(T, dtype=jnp.float32)[:, None] * inv[None, :]
    return jnp.cos(ang), jnp.sin(ang)


def apply_partial_rope(x, cos, sin):
    c = cos[None, :, None, None, :].astype(x.dtype)
    s = sin[None, :, None, None, :].astype(x.dtype)
    half = ROPE_DIM // 2
    x1 = x[..., :half]
    x2 = x[..., half:ROPE_DIM]
    return jnp.concatenate([x1 * c - x2 * s, x2 * c + x1 * s, x[..., ROPE_DIM:]], axis=-1)


def diff_attention_branch(p, cos, sin, q_norm_g, k_norm_g, lq1, lk1, lq2, lk2, subln_g, lam_init):
    B, T, _ = p.shape
    q = p[..., :DA_QK_WIDTH].reshape(B, T, DA_HEADS, 2, DA_HEAD_DIM)
    k = p[..., DA_QK_WIDTH:2 * DA_QK_WIDTH].reshape(B, T, DA_HEADS, 2, DA_HEAD_DIM)
    v = p[..., 2 * DA_QK_WIDTH:].reshape(B, T, DA_HEADS, DA_V_DIM)
    q = apply_partial_rope(rms_norm(q, q_norm_g), cos, sin)
    k = apply_partial_rope(rms_norm(k, k_norm_g), cos, sin)
    lam = (jnp.exp(jnp.sum(lq1.astype(jnp.float32) * lk1.astype(jnp.float32)))
           - jnp.exp(jnp.sum(lq2.astype(jnp.float32) * lk2.astype(jnp.float32))) + lam_init)
    scale = DA_HEAD_DIM ** -0.5
    qh = q.transpose(0, 2, 3, 1, 4)
    kh = k.transpose(0, 2, 3, 1, 4)
    vh = v.transpose(0, 2, 1, 3)
    nb = T // Q_BLOCK
    qb = qh.reshape(B, DA_HEADS, 2, nb, Q_BLOCK, DA_HEAD_DIM).transpose(3, 0, 1, 2, 4, 5)

    def block(qi):
        s = jnp.einsum('bhsqd,bhskd->bhsqk', qi, kh).astype(jnp.float32) * scale
        pr = jax.nn.softmax(s, axis=-1)
        a = pr[:, :, 0] - lam * pr[:, :, 1]
        return jnp.einsum('bhqk,bhkv->bhqv', a.astype(vh.dtype), vh)

    o = lax.map(block, qb)
    o = o.transpose(1, 0, 3, 2, 4).reshape(B, T, DA_HEADS, DA_V_DIM)
    o = rms_norm(o, subln_g) * (1.0 - lam_init)
    return o.reshape(B, T, DA_V_WIDTH)


def centred_shift(p):
    prev = jnp.pad(p[:, :-1], ((0, 0), (1, 0), (0, 0)))
    nxt = jnp.pad(p[:, 1:], ((0, 0), (0, 1), (0, 0)))
    return 0.5 * (prev + nxt)


def wkv7_scan(r, w, k, v, a, b, reverse):
    B, T, H, N = r.shape
    xs = tuple(t.astype(jnp.float32).transpose(1, 0, 2, 3) for t in (r, w, k, v, a, b))

    def step(S, inp):
        r_t, w_t, k_t, v_t, a_t, b_t = inp
        sa = jnp.einsum('bhvk,bhk->bhv', S, a_t)
        S = S * w_t[:, :, None, :] + sa[..., None] * b_t[:, :, None, :] + v_t[..., None] * k_t[:, :, None, :]
        return S, jnp.einsum('bhvk,bhk->bhv', S, r_t)

    S0 = jnp.zeros((B, H, N, N), jnp.float32)
    _, y = lax.scan(step, S0, xs, reverse=reverse)
    return y.transpose(1, 0, 2, 3)


def rwkv7_branch(p, shift_mu, w0_f, w2_f, w0_b, w2_b, a0_f, a2_f, a0_b, a2_b, g2,
                 k_k, k_a, r_k, ln_x_w, ln_x_b):
    B, T, _ = p.shape
    dt = p.dtype
    p = p + shift_mu * (centred_shift(p) - p)
    r, k, v, wd_f, wd_b, ad_f, ad_b, gd = jnp.split(p, RW_SPLITS, axis=-1)

    def decay(w0, wd, w2):
        wl = (w0 + jnp.tanh(wd) @ w2).astype(jnp.float32)
        return jnp.exp(-jnp.exp(-jax.nn.softplus(-wl) - 0.5))

    dec_f = decay(w0_f, wd_f, w2_f)
    dec_b = decay(w0_b, wd_b, w2_b)
    a_f = jax.nn.sigmoid(a0_f + ad_f @ a2_f)
    a_b = jax.nn.sigmoid(a0_b + ad_b @ a2_b)
    g = jax.nn.sigmoid(gd) @ g2

    hs = lambda t: t.reshape(B, T, RW_HEADS, RW_HEAD_DIM)
    r, k, v, a_f, a_b, dec_f, dec_b = map(hs, (r, k, v, a_f, a_b, dec_f, dec_b))
    kk = (k * k_k.reshape(RW_HEADS, RW_HEAD_DIM)).astype(jnp.float32)
    kk = kk / jnp.maximum(jnp.sqrt(jnp.sum(kk * kk, axis=-1, keepdims=True)), 1e-12)
    k_ah = k_a.reshape(RW_HEADS, RW_HEAD_DIM)
    k_f = k * (1 + (a_f - 1) * k_ah)
    k_b = k * (1 + (a_b - 1) * k_ah)
    y = (wkv7_scan(r, dec_f, k_f, v, -kk, kk * a_f, reverse=False)
         + wkv7_scan(r, dec_b, k_b, v, -kk, kk * a_b, reverse=True))
    mean = jnp.mean(y, axis=-1, keepdims=True)
    var = jnp.mean(jnp.square(y - mean), axis=-1, keepdims=True)
    y = ((y - mean) * lax.rsqrt(var + GN_EPS) * ln_x_w.reshape(RW_HEADS, RW_HEAD_DIM).astype(jnp.float32)
         + ln_x_b.reshape(RW_HEADS, RW_HEAD_DIM).astype(jnp.float32)).astype(dt)
    bonus = jnp.sum(r * (k_f + k_b) * r_k, axis=-1, keepdims=True) * v
    return (y + bonus).reshape(B, T, RW_WIDTH) * g


def expert_choice_ffn(hn, w_router, w_gate_e, w_up_e, w_down_e):
    B, T, D = hn.shape
    cap = CAPACITY_FACTOR * T // N_EXPERTS
    aff = jax.nn.softmax((hn @ w_router).astype(jnp.float32), axis=-1)
    gate, idx = lax.top_k(aff.transpose(0, 2, 1), cap)
    xe = jax.vmap(lambda h, i: h[i])(hn, idx)
    hid = jax.nn.silu(jnp.einsum('becd,edf->becf', xe, w_gate_e)) * jnp.einsum('becd,edf->becf', xe, w_up_e)
    ye = jnp.einsum('becf,efd->becd', hid, w_down_e) * gate[..., None].astype(hn.dtype)
    flat = (jnp.arange(B, dtype=jnp.int32)[:, None, None] * T + idx).reshape(-1)
    out = jnp.zeros((B * T, D), hn.dtype).at[flat].add(ye.reshape(-1, D))
    return out.reshape(B, T, D)


def setup_inputs(seed: int = 0) -> dict:
    key = jax.random.key(seed)
    ks = iter(jax.random.split(key, 40))
    f32 = jnp.float32
    nrm = lambda shape, s: jax.random.normal(next(ks), shape, f32) * s
    gain = lambda shape: 1.0 + 0.1 * jax.random.normal(next(ks), shape, f32)
    L = DEPTH
    return {
        "x": jax.random.normal(next(ks), (BATCH, SEQ, D_MODEL), f32),
        "attn_norm_g": gain((L, D_MODEL)),
        "w_in": nrm((L, D_MODEL, IN_COLS), D_MODEL ** -0.5),
        "q_norm_g": gain((L, DA_HEAD_DIM)),
        "k_norm_g": gain((L, DA_HEAD_DIM)),
        "lambda_q1": nrm((L, DA_HEAD_DIM), 0.1),
        "lambda_k1": nrm((L, DA_HEAD_DIM), 0.1),
        "lambda_q2": nrm((L, DA_HEAD_DIM), 0.1),
        "lambda_k2": nrm((L, DA_HEAD_DIM), 0.1),
        "subln_g": gain((L, DA_V_DIM)),
        "shift_mu": jax.random.uniform(next(ks), (L, RW_COLS), f32),
        "w0_f": jax.random.uniform(next(ks), (L, RW_WIDTH), f32, -5.0, -0.5),
        "w2_f": nrm((L, DECAY_LORA, RW_WIDTH), 0.5 * DECAY_LORA ** -0.5),
        "w0_b": jax.random.uniform(next(ks), (L, RW_WIDTH), f32, -5.0, -0.5),
        "w2_b": nrm((L, DECAY_LORA, RW_WIDTH), 0.5 * DECAY_LORA ** -0.5),
        "a0_f": nrm((L, RW_WIDTH), 0.5),
        "a2_f": nrm((L, AAA_LORA, RW_WIDTH), 0.5 * AAA_LORA ** -0.5),
        "a0_b": nrm((L, RW_WIDTH), 0.5),
        "a2_b": nrm((L, AAA_LORA, RW_WIDTH), 0.5 * AAA_LORA ** -0.5),
        "g2": nrm((L, GATE_LORA, RW_WIDTH), GATE_LORA ** -0.5),
        "k_k": 0.85 + 0.1 * jax.random.normal(next(ks), (L, RW_WIDTH), f32),
        "k_a": gain((L, RW_WIDTH)),
        "r_k": nrm((L, RW_HEADS, RW_HEAD_DIM), 0.1),
        "ln_x_w": gain((L, RW_WIDTH)),
        "ln_x_b": nrm((L, RW_WIDTH), 0.02),
        "w_branch_a": nrm((L, DA_V_WIDTH, D_MODEL), DA_V_WIDTH ** -0.5),
        "w_branch_b": nrm((L, RW_WIDTH, D_MODEL), RW_WIDTH ** -0.5),
        "w_out": nrm((L, D_MODEL, D_MODEL), D_MODEL ** -0.5),
        "ffn_norm_g": gain((L, D_MODEL)),
        "w_router": nrm((L, D_MODEL, N_EXPERTS), D_MODEL ** -0.5),
        "w_gate_e": nrm((L, N_EXPERTS, D_MODEL, EXPERT_FF), D_MODEL ** -0.5),
        "w_up_e": nrm((L, N_EXPERTS, D_MODEL, EXPERT_FF), D_MODEL ** -0.5),
        "w_down_e": nrm((L, N_EXPERTS, EXPERT_FF, D_MODEL), EXPERT_FF ** -0.5),
    }


def reference(x, attn_norm_g, w_in, q_norm_g, k_norm_g, lambda_q1, lambda_k1, lambda_q2, lambda_k2,
              subln_g, shift_mu, w0_f, w2_f, w0_b, w2_b, a0_f, a2_f, a0_b, a2_b, g2, k_k, k_a, r_k,
              ln_x_w, ln_x_b, w_branch_a, w_branch_b, w_out, ffn_norm_g, w_router,
              w_gate_e, w_up_e, w_down_e):
    B, T, _ = x.shape
    cos, sin = rope_tables(T)
    h = x
    for l in range(DEPTH):
        lam_init = 0.8 - 0.6 * math.exp(-0.3 * l)
        hn = rms_norm(h, attn_norm_g[l])
        proj = hn @ w_in[l]
        p_da = proj[..., :DA_COLS]
        p_rw = proj[..., DA_COLS:DA_COLS + RW_COLS]
        p_gate = proj[..., DA_COLS + RW_COLS:]
        y_a = diff_attention_branch(p_da, cos, sin, q_norm_g[l], k_norm_g[l], lambda_q1[l], lambda_k1[l],
                                    lambda_q2[l], lambda_k2[l], subln_g[l], lam_init)
        y_b = rwkv7_branch(p_rw, shift_mu[l], w0_f[l], w2_f[l], w0_b[l], w2_b[l], a0_f[l], a2_f[l],
                           a0_b[l], a2_b[l], g2[l], k_k[l], k_a[l], r_k[l], ln_x_w[l], ln_x_b[l])
        gates = jax.nn.sigmoid(p_gate)
        gate_a = gates[..., :D_MODEL]
        gate_b = gates[..., D_MODEL:]
        merged = gate_a * (y_a @ w_branch_a[l]) + gate_b * (y_b @ w_branch_b[l])
        h = h + merged @ w_out[l]
        h = h + expert_choice_ffn(rms_norm(h, ffn_norm_g[l]), w_router[l], w_gate_e[l], w_up_e[l], w_down_e[l])
    return h
```

```python
import functools
import math

import jax
import jax.numpy as jnp
from jax import lax
from jax.experimental import pallas as pl
from jax.experimental.pallas import tpu as pltpu

F32 = jnp.float32
BF16 = jnp.bfloat16

LANES = 128
HEAD = 64
CHUNK = 64
N_EXPERTS = 16
RMS_EPS = 1e-6
GN_EPS = 64e-5
ROPE_THETA = 500000.0
ROPE_DIM = 16
VMEM_LIMIT = 56 * 1024 * 1024


def _cparams(*sem):
    return pltpu.CompilerParams(dimension_semantics=sem, vmem_limit_bytes=VMEM_LIMIT)


def _dot(a, b):
    return jnp.dot(a, b, preferred_element_type=F32)


def _dot_nt(a, b):
    return lax.dot_general(a, b, (((1,), (1,)), ((), ())), preferred_element_type=F32)


def _dot_tn(a, b):
    return lax.dot_general(a, b, (((0,), (0,)), ((), ())), preferred_element_type=F32)


def _split3(x):
    h1 = x.astype(BF16)
    r1 = x - h1.astype(F32)
    h2 = r1.astype(BF16)
    h3 = (r1 - h2.astype(F32)).astype(BF16)
    return h1, h2, h3


def _dot_exact_lhs(w_bf16, x):
    h1, h2, h3 = _split3(x)
    return _dot(w_bf16, h1) + _dot(w_bf16, h2) + _dot(w_bf16, h3)


def _dot_exact_rhs(x, w_bf16):
    h1, h2, h3 = _split3(x)
    return _dot(h1, w_bf16) + _dot(h2, w_bf16) + _dot(h3, w_bf16)


def _scan_direction(r_ref, v_ref, kk_ref, lw_ref, k_ref, a_ref, y_ref, s_ref, d, reverse):
    L = CHUNK
    n_pairs = r_ref.shape[-1] // LANES
    t_i = lax.broadcasted_iota(jnp.int32, (L, L), 0)
    i_i = lax.broadcasted_iota(jnp.int32, (L, L), 1)
    tri = jnp.where((i_i >= t_i) if reverse else (i_i <= t_i), 1.0, 0.0).astype(BF16)

    lw = lw_ref[...]
    cum = _dot_exact_lhs(tri, lw)
    last = 0 if reverse else L - 1
    tot = cum[last:last + 1, :]
    winv = jnp.exp(-cum)
    wrem = jnp.exp(tot - cum)
    kk = kk_ref[...]
    b = kk * a_ref[...]
    k = k_ref[...]
    rt = r_ref[...] * jnp.exp(cum)
    at = -kk * jnp.exp(cum - lw)
    bt = (b * winv).astype(BF16)
    kt = (k * winv).astype(BF16)
    bh = (b * wrem).astype(BF16)
    kh = (k * wrem).astype(BF16)
    w_tot = jnp.exp(tot)
    v = v_ref[...].astype(BF16)

    lane = lax.broadcasted_iota(jnp.int32, (1, LANES), 1)
    lo = lane < HEAD
    t2 = lax.broadcasted_iota(jnp.int32, (L, LANES), 0)
    i2 = lax.broadcasted_iota(jnp.int32, (L, LANES), 1) % HEAD
    strict = (i2 > t2) if reverse else (i2 < t2)
    incl = (i2 >= t2) if reverse else (i2 <= t2)
    rr = lax.broadcasted_iota(jnp.int32, (LANES, LANES), 0) < HEAD
    cc = lax.broadcasted_iota(jnp.int32, (LANES, LANES), 1) < HEAD
    same_head = rr == cc
    zeros = jnp.zeros((L, LANES), BF16)

    for p in range(n_pairs):
        sl = slice(p * LANES, (p + 1) * LANES)
        at_p, rt_p = at[:, sl], rt[:, sl]
        a_lo = jnp.where(lo, at_p, 0.0).astype(BF16)
        a_hi = jnp.where(lo, 0.0, at_p).astype(BF16)
        r_lo = jnp.where(lo, rt_p, 0.0).astype(BF16)
        r_hi = jnp.where(lo, 0.0, rt_p).astype(BF16)
        bt_p, kt_p, v_p = bt[:, sl], kt[:, sl], v[:, sl]
        g0 = _dot_nt(jnp.concatenate([a_lo, r_lo], axis=0), jnp.concatenate([bt_p, kt_p], axis=0))
        g1 = _dot_nt(jnp.concatenate([a_hi, r_hi], axis=0), jnp.concatenate([kt_p, bt_p], axis=0))
        m0 = jnp.where(strict, g0[:L], 0.0)
        n0 = jnp.where(incl, g0[L:], 0.0)
        m1 = jnp.where(strict, g1[:L], 0.0)
        n1 = jnp.where(incl, g1[L:], 0.0)
        s = s_ref[d, p]
        ars = _dot_nt(jnp.concatenate([a_lo, a_hi, r_lo, r_hi], axis=0), s.astype(BF16))
        akv0 = _dot(m0.astype(BF16), jnp.concatenate([zeros, v_p], axis=0))
        akv1 = _dot(m1.astype(BF16), jnp.concatenate([v_p, zeros], axis=0))
        x = ars[:2 * L] + jnp.concatenate([akv0, akv1], axis=0)
        pw = jnp.concatenate([jnp.where(lo, m0, 0.0), jnp.where(lo, 0.0, m1)], axis=0).astype(BF16)
        n_sq = int(math.log2(L))
        for j in range(n_sq):
            x = x + _dot(pw, x.astype(BF16))
            if j + 1 < n_sq:
                pw = _dot(pw, pw).astype(BF16)
        u0, u1 = x[:L], x[L:]
        y0 = ars[2 * L:3 * L] + _dot(n0.astype(BF16), jnp.concatenate([u0.astype(BF16), v_p], axis=0))
        y1 = ars[3 * L:] + _dot(n1.astype(BF16), jnp.concatenate([v_p, u1.astype(BF16)], axis=0))
        y_ref[:, sl] = jnp.where(lo, y0, y1)
        u_p = jnp.where(lo, u0, u1).astype(BF16)
        ds = _dot_tn(jnp.concatenate([u_p, v_p], axis=0), jnp.concatenate([bh[:, sl], kh[:, sl]], axis=0))
        s_ref[d, p] = s * w_tot[:, sl] + jnp.where(same_head, ds, 0.0)


def _rwkv_scan_kernel(r_f, v_f, kk_f, lw_f, k_f, a_f, r_b, v_b, kk_b, lw_b, k_b, a_b, yf_ref, yb_ref, s_ref):
    @pl.when(pl.program_id(1) == 0)
    def _():
        s_ref[...] = jnp.zeros_like(s_ref)

    _scan_direction(r_f, v_f, kk_f, lw_f, k_f, a_f, yf_ref, s_ref, 0, False)
    _scan_direction(r_b, v_b, kk_b, lw_b, k_b, a_b, yb_ref, s_ref, 1, True)


def rwkv_scan(r, v, kk, lw_f, k_f, a_f, lw_b, k_b, a_b, batch):
    m, c = r.shape
    nc = m // batch // CHUNK
    fwd = pl.BlockSpec((CHUNK, c), lambda bi, ci: (bi * nc + ci, 0))
    bwd = pl.BlockSpec((CHUNK, c), lambda bi, ci: (bi * nc + nc - 1 - ci, 0))
    return pl.pallas_call(
        _rwkv_scan_kernel,
        grid=(batch, nc),
        in_specs=[fwd] * 6 + [bwd] * 6,
        out_specs=[fwd, bwd],
        out_shape=[jax.ShapeDtypeStruct((m, c), F32)] * 2,
        scratch_shapes=[pltpu.VMEM((2, c // LANES, LANES, LANES), F32)],
        compiler_params=_cparams("arbitrary", "arbitrary"),
    )(r, v, kk, lw_f, k_f, a_f, r, v, kk, lw_b, k_b, a_b)


def _sigmoid(x):
    return 1.0 / (1.0 + jnp.exp(-x))


def _split2(x):
    hi = x.astype(BF16)
    return hi, (x - hi.astype(F32)).astype(BF16)


def _dot_f32(a, w_hi, w_lo):
    a_hi, a_lo = _split2(a)
    return _dot(a_hi, w_hi) + _dot(a_lo, w_hi) + _dot(a_hi, w_lo)


def _head_block_ones():
    r = lax.broadcasted_iota(jnp.int32, (LANES, LANES), 0) // HEAD
    c = lax.broadcasted_iota(jnp.int32, (LANES, LANES), 1) // HEAD
    return jnp.where(r == c, 1.0, 0.0).astype(BF16)


def _group_sum(x, ones):
    tiles = [_dot_exact_rhs(x[:, j * LANES:(j + 1) * LANES], ones) for j in range(x.shape[1] // LANES)]
    return tiles[0] if len(tiles) == 1 else jnp.concatenate(tiles, axis=1)


def _rmsnorm_kernel(x_ref, g_ref, o_ref):
    x = x_ref[...]
    y = x * lax.rsqrt(jnp.mean(x * x, axis=-1, keepdims=True) + RMS_EPS)
    o_ref[...] = (y * g_ref[...]).astype(o_ref.dtype)


def rmsnorm_bf16(x, g, tm):
    m, d = x.shape
    return pl.pallas_call(
        _rmsnorm_kernel,
        grid=(m // tm,),
        in_specs=[pl.BlockSpec((tm, d), lambda i: (i, 0)), pl.BlockSpec((1, d), lambda i: (0, 0))],
        out_specs=pl.BlockSpec((tm, d), lambda i: (i, 0)),
        out_shape=jax.ShapeDtypeStruct((m, d), BF16),
        compiler_params=_cparams("parallel"),
    )(x, g.reshape(1, d))


def _matmul_kernel(a_ref, w_ref, o_ref, *, act):
    acc = _dot(a_ref[...], w_ref[...])
    if act == "sigmoid":
        acc = _sigmoid(acc)
    o_ref[...] = acc.astype(o_ref.dtype)


def matmul(a, w, out_dtype, tm, tn, act=None):
    m, k = a.shape
    n = w.shape[1]
    return pl.pallas_call(
        functools.partial(_matmul_kernel, act=act),
        grid=(n // tn, m // tm),
        in_specs=[pl.BlockSpec((tm, k), lambda j, i: (i, 0)), pl.BlockSpec((k, tn), lambda j, i: (0, j))],
        out_specs=pl.BlockSpec((tm, tn), lambda j, i: (i, j)),
        out_shape=jax.ShapeDtypeStruct((m, n), out_dtype),
        compiler_params=_cparams("parallel", "parallel"),
    )(a, w)


def _qk_prep_kernel(qk_ref, cos_ref, sin_a_ref, sin_b_ref, gq_ref, gk_ref, q_ref, k_ref):
    ones = _head_block_ones()
    cos, sin_a, sin_b = cos_ref[...], sin_a_ref[...], sin_b_ref[...]
    n_tiles = q_ref.shape[1] // LANES
    for j in range(2 * n_tiles):
        x = qk_ref[:, j * LANES:(j + 1) * LANES]
        ms = _group_sum(x * x, ones) * (1.0 / HEAD)
        is_q = j < n_tiles
        y = x * lax.rsqrt(ms + RMS_EPS) * (gq_ref[...] if is_q else gk_ref[...])
        y = y * cos + pltpu.roll(y, LANES - ROPE_DIM // 2, 1) * sin_a + pltpu.roll(y, ROPE_DIM // 2, 1) * sin_b
        if is_q:
            q_ref[:, j * LANES:(j + 1) * LANES] = (y * HEAD ** -0.5).astype(q_ref.dtype)
        else:
            jj = j - n_tiles
            k_ref[:, jj * LANES:(jj + 1) * LANES] = y.astype(k_ref.dtype)


def qk_prep(qk, cos, sin_a, sin_b, gq, gk, seq, tm):
    m, w2 = qk.shape
    w = w2 // 2
    st = seq // tm
    tab = pl.BlockSpec((tm, LANES), lambda i: (i % st, 0))
    vec = pl.BlockSpec((1, LANES), lambda i: (0, 0))
    return pl.pallas_call(
        _qk_prep_kernel,
        grid=(m // tm,),
        in_specs=[pl.BlockSpec((tm, w2), lambda i: (i, 0)), tab, tab, tab, vec, vec],
        out_specs=[pl.BlockSpec((tm, w), lambda i: (i, 0))] * 2,
        out_shape=[jax.ShapeDtypeStruct((m, w), BF16)] * 2,
        compiler_params=_cparams("parallel"),
    )(qk, cos, sin_a, sin_b, gq, gk)


def _attn_kernel(lam_ref, q_ref, k_ref, v_ref, g_ref, o_ref, *, lam_init):
    lp = lam_ref[...]
    lam = (jnp.exp(jnp.sum(lp[0:1] * lp[1:2], axis=-1, keepdims=True))
           - jnp.exp(jnp.sum(lp[2:3] * lp[3:4], axis=-1, keepdims=True)) + lam_init)
    q = q_ref[...]
    k = k_ref[...]
    v = v_ref[...]
    lo = lax.broadcasted_iota(jnp.int32, (1, LANES), 1) < HEAD

    def branch(qm):
        s = _dot_nt(qm, k)
        p = jnp.exp(s - jnp.max(s, axis=-1, keepdims=True))
        return _dot(p.astype(BF16), v) / jnp.sum(p, axis=-1, keepdims=True)

    o = branch(jnp.where(lo, q, jnp.zeros_like(q))) - lam * branch(jnp.where(lo, jnp.zeros_like(q), q))
    o = o * lax.rsqrt(jnp.mean(o * o, axis=-1, keepdims=True) + RMS_EPS)
    o_ref[...] = (o * g_ref[...] * (1.0 - lam_init)).astype(o_ref.dtype)


def diff_attention(q, k, v, lam_params, subln_g, batch, lam_init, tq):
    m, w = q.shape
    seq = m // batch
    nq = seq // tq
    return pl.pallas_call(
        functools.partial(_attn_kernel, lam_init=lam_init),
        grid=(batch, w // LANES, nq),
        in_specs=[pl.BlockSpec((4, HEAD), lambda b, h, i: (0, 0)),
                  pl.BlockSpec((tq, LANES), lambda b, h, i: (b * nq + i, h)),
                  pl.BlockSpec((seq, LANES), lambda b, h, i: (b, h)),
                  pl.BlockSpec((seq, LANES), lambda b, h, i: (b, h)),
                  pl.BlockSpec((1, LANES), lambda b, h, i: (0, 0))],
        out_specs=pl.BlockSpec((tq, LANES), lambda b, h, i: (b * nq + i, h)),
        out_shape=jax.ShapeDtypeStruct((m, w), BF16),
        compiler_params=_cparams("parallel", "parallel", "parallel"),
    )(lam_params, q, k, v, subln_g.reshape(1, LANES))


def _rwkv_prep_kernel(p_ref, prev_ref, next_ref, mu_ref, w0_ref, w2h_ref, w2l_ref, a0_ref, a2h_ref, a2l_ref,
                      g2h_ref, g2l_ref, kkw_ref, ka_ref, rk_ref,
                      r_o, v_o, kk_o, kf_o, kb_o, lwf_o, lwb_o, af_o, ab_o, bonus_o, g_o, *, seq_tiles):
    tm = p_ref.shape[0]
    c = r_o.shape[1]
    ti = pl.program_id(0) % seq_tiles
    p = p_ref[...]
    row = lax.broadcasted_iota(jnp.int32, (tm, 1), 0)
    prev_row = jnp.where(ti > 0, prev_ref[7:8, :], 0.0)
    next_row = jnp.where(ti < seq_tiles - 1, next_ref[0:1, :], 0.0)
    prev = jnp.where(row == 0, prev_row, pltpu.roll(p, 1, 0))
    nxt = jnp.where(row == tm - 1, next_row, pltpu.roll(p, tm - 1, 0))
    p = p + mu_ref[...] * (0.5 * (prev + nxt) - p)

    r, k, v = p[:, :c], p[:, c:2 * c], p[:, 2 * c:3 * c]
    wd = p[:, 3 * c:3 * c + LANES]
    ad = p[:, 3 * c + LANES:3 * c + 2 * LANES]
    gd = p[:, 3 * c + 2 * LANES:]
    wl = w0_ref[...] + _dot_f32(jnp.tanh(wd), w2h_ref[...], w2l_ref[...])
    lw = -math.exp(-0.5) * _sigmoid(wl)
    a = _sigmoid(a0_ref[...] + _dot_f32(ad, a2h_ref[...], a2l_ref[...]))
    g_o[...] = _dot_f32(_sigmoid(gd), g2h_ref[...], g2l_ref[...])
    a_f, a_b = a[:, :c], a[:, c:]
    ones = _head_block_ones()
    kk = k * kkw_ref[...]
    kk = kk / jnp.maximum(jnp.sqrt(_group_sum(kk * kk, ones)), 1e-12)
    k_a = ka_ref[...]
    k_f = k * (1.0 + (a_f - 1.0) * k_a)
    k_b = k * (1.0 + (a_b - 1.0) * k_a)
    bonus_o[...] = _group_sum(r * (k_f + k_b) * rk_ref[...], ones) * v
    r_o[...] = r
    v_o[...] = v
    kk_o[...] = kk
    kf_o[...] = k_f
    kb_o[...] = k_b
    lwf_o[...] = lw[:, :c]
    lwb_o[...] = lw[:, c:]
    af_o[...] = a_f
    ab_o[...] = a_b


def rwkv_prep(p, mu, w0, w2, a0, a2, g2, k_k, k_a, r_k, seq, tm, c):
    m, pc = p.shape
    st = seq // tm
    hb = tm // 8
    last = m // 8 - 1
    full = lambda a: pl.BlockSpec(a.shape, lambda i: (0, 0))
    row_c = pl.BlockSpec((tm, c), lambda i: (i, 0))
    small = [mu, w0, w2[0], w2[1], a0, a2[0], a2[1], g2[0], g2[1], k_k, k_a, r_k]
    return pl.pallas_call(
        functools.partial(_rwkv_prep_kernel, seq_tiles=st),
        grid=(m // tm,),
        in_specs=[pl.BlockSpec((tm, pc), lambda i: (i, 0)),
                  pl.BlockSpec((8, pc), lambda i: (jnp.maximum(i * hb - 1, 0), 0)),
                  pl.BlockSpec((8, pc), lambda i: (jnp.minimum((i + 1) * hb, last), 0))] + [full(a) for a in small],
        out_specs=[row_c] * 11,
        out_shape=[jax.ShapeDtypeStruct((m, c), F32)] * 11,
        compiler_params=_cparams("parallel"),
    )(p, p, p, *small)


def _rwkv_post_kernel(yf_ref, yb_ref, bonus_ref, g_ref, lnw_ref, lnb_ref, o_ref):
    ones = _head_block_ones()
    y = yf_ref[...] + yb_ref[...]
    mean = _group_sum(y, ones) * (1.0 / HEAD)
    yc = y - mean
    var = _group_sum(yc * yc, ones) * (1.0 / HEAD)
    yn = yc * lax.rsqrt(var + GN_EPS) * lnw_ref[...] + lnb_ref[...]
    o_ref[...] = ((yn + bonus_ref[...]) * g_ref[...]).astype(o_ref.dtype)


def rwkv_post(yf, yb, bonus, g, ln_w, ln_b, tm):
    m, c = yf.shape
    row = pl.BlockSpec((tm, c), lambda i: (i, 0))
    vec = pl.BlockSpec((1, c), lambda i: (0, 0))
    return pl.pallas_call(
        _rwkv_post_kernel,
        grid=(m // tm,),
        in_specs=[row] * 4 + [vec] * 2,
        out_specs=row,
        out_shape=jax.ShapeDtypeStruct((m, c), BF16),
        compiler_params=_cparams("parallel"),
    )(yf, yb, bonus, g, ln_w.reshape(1, c), ln_b.reshape(1, c))


def _merge_kernel(ya_ref, yb_ref, gate_ref, x_ref, wa_ref, wb_ref, wo_ref, fg_ref, wrh_ref, wrl_ref,
                  h_ref, hn_ref, aff_ref):
    d = x_ref.shape[1]
    merged = (gate_ref[:, :d].astype(F32) * _dot(ya_ref[...], wa_ref[...])
              + gate_ref[:, d:].astype(F32) * _dot(yb_ref[...], wb_ref[...]))
    h = x_ref[...] + _dot(merged.astype(BF16), wo_ref[...])
    h_ref[...] = h
    hn = h * lax.rsqrt(jnp.mean(h * h, axis=-1, keepdims=True) + RMS_EPS) * fg_ref[...]
    hn_ref[...] = hn.astype(hn_ref.dtype)
    hn_hi, hn_lo = _split2(hn)
    wr_hi = wrh_ref[...]
    logits = _dot_nt(wr_hi, hn_hi) + _dot_nt(wr_hi, hn_lo) + _dot_nt(wrl_ref[...], hn_hi)
    e = jnp.exp(logits - jnp.max(logits, axis=0, keepdims=True))
    aff_ref[...] = e / jnp.sum(e, axis=0, keepdims=True)


def merge_out_router(ya, yb, gates, x, wa, wb, wo, ffn_g, wr_hi, wr_lo, tm):
    m, d = x.shape
    full = lambda a: pl.BlockSpec(a.shape, lambda i: (0, 0))
    row = lambda a: pl.BlockSpec((tm, a.shape[1]), lambda i: (i, 0))
    ne = wr_hi.shape[0]
    return pl.pallas_call(
        _merge_kernel,
        grid=(m // tm,),
        in_specs=[row(ya), row(yb), row(gates), row(x), full(wa), full(wb), full(wo), full(ffn_g),
                  full(wr_hi), full(wr_lo)],
        out_specs=[pl.BlockSpec((tm, d), lambda i: (i, 0)), pl.BlockSpec((tm, d), lambda i: (i, 0)),
                   pl.BlockSpec((ne, tm), lambda i: (0, i))],
        out_shape=[jax.ShapeDtypeStruct((m, d), F32), jax.ShapeDtypeStruct((m, d), BF16),
                   jax.ShapeDtypeStruct((ne, m), F32)],
        compiler_params=_cparams("parallel"),
    )(ya, yb, gates, x, wa, wb, wo, ffn_g, wr_hi, wr_lo)


def _prefix_excl(m_bf16, upper):
    rows, t = m_bf16.shape
    off = jnp.zeros((rows, 1), F32)
    out = []
    for j in range(t // LANES):
        blk = m_bf16[:, j * LANES:(j + 1) * LANES]
        out.append(_dot(blk, upper) + off)
        off = off + jnp.sum(blk.astype(F32), axis=-1, keepdims=True)
    return jnp.concatenate(out, axis=1)


def _select_kernel(aff_ref, pos_ref, *, cap):
    bits = pltpu.bitcast(aff_ref[...], jnp.int32)

    def step(i, thr):
        cand = thr | (jnp.int32(1) << (30 - i))
        cnt = jnp.sum(jnp.where(bits >= cand, 1.0, 0.0), axis=-1, keepdims=True)
        return jnp.where(cnt >= cap, cand, thr)

    thr = lax.fori_loop(0, 31, step, jnp.zeros((bits.shape[0], 1), jnp.int32))
    gt = bits > thr
    eq = bits == thr
    r = lax.broadcasted_iota(jnp.int32, (LANES, LANES), 0)
    c = lax.broadcasted_iota(jnp.int32, (LANES, LANES), 1)
    upper = jnp.where(r < c, 1.0, 0.0).astype(BF16)
    need = cap - jnp.sum(jnp.where(gt, 1.0, 0.0), axis=-1, keepdims=True)
    eq_rank = _prefix_excl(jnp.where(eq, 1.0, 0.0).astype(BF16), upper)
    sel = gt | (eq & (eq_rank < need))
    pos = _prefix_excl(jnp.where(sel, 1.0, 0.0).astype(BF16), upper)
    pos_ref[...] = jnp.where(sel, pos, -1.0).astype(jnp.int32)


def select_slots(aff_t, batch, cap):
    ne, m = aff_t.shape
    seq = m // batch
    return pl.pallas_call(
        functools.partial(_select_kernel, cap=cap),
        grid=(batch,),
        in_specs=[pl.BlockSpec((ne, seq), lambda b: (0, b))],
        out_specs=pl.BlockSpec((ne, seq), lambda b: (0, b)),
        out_shape=jax.ShapeDtypeStruct((ne, m), jnp.int32),
        compiler_params=_cparams("parallel"),
    )(aff_t)


TOKEN_CHUNK = 1024


def _one_hot(pos_chunk, cap):
    slot = lax.broadcasted_iota(jnp.int32, (cap, pos_chunk.shape[1]), 0)
    return pos_chunk == slot


def _gather_kernel(pos_ref, aff_ref, hn_ref, xe_ref, gate_ref):
    cap = xe_ref.shape[2]
    seq = hn_ref.shape[0]
    tc = min(TOKEN_CHUNK, seq)
    xe = jnp.zeros(xe_ref.shape[2:], F32)
    gate = jnp.zeros((cap, 1), F32)
    for j in range(seq // tc):
        hit = _one_hot(pos_ref[0, :, j * tc:(j + 1) * tc], cap)
        xe = xe + _dot(jnp.where(hit, 1.0, 0.0).astype(BF16), hn_ref[j * tc:(j + 1) * tc, :])
        gate = gate + jnp.sum(jnp.where(hit, aff_ref[0, :, j * tc:(j + 1) * tc], 0.0), axis=-1, keepdims=True)
    xe_ref[0, 0] = xe.astype(xe_ref.dtype)
    gate_ref[0, 0] = gate


def moe_gather(pos3, aff3, hn, batch, cap):
    ne = pos3.shape[0]
    m, d = hn.shape
    seq = m // batch
    row = pl.BlockSpec((1, 1, seq), lambda b, e: (e, 0, b))
    return pl.pallas_call(
        _gather_kernel,
        grid=(batch, ne),
        in_specs=[row, row, pl.BlockSpec((seq, d), lambda b, e: (b, 0))],
        out_specs=[pl.BlockSpec((1, 1, cap, d), lambda b, e: (e, b, 0, 0)),
                   pl.BlockSpec((1, 1, cap, 1), lambda b, e: (e, b, 0, 0))],
        out_shape=[jax.ShapeDtypeStruct((ne, batch, cap, d), BF16),
                   jax.ShapeDtypeStruct((ne, batch, cap, 1), F32)],
        compiler_params=_cparams("parallel", "parallel"),
    )(pos3, aff3, hn)


def _expert_kernel(xe_ref, gate_ref, wg_ref, wu_ref, hid_ref):
    xe = xe_ref[0, 0]
    hg = _dot(xe, wg_ref[0])
    hu = _dot(xe, wu_ref[0])
    hid_ref[0, 0] = (hg * _sigmoid(hg) * hu * gate_ref[0, 0]).astype(hid_ref.dtype)


def moe_experts(xe, gate, wg, wu):
    ne, batch, cap, d = xe.shape
    ff = wg.shape[2]
    return pl.pallas_call(
        _expert_kernel,
        grid=(ne, batch),
        in_specs=[pl.BlockSpec((1, 1, cap, d), lambda e, b: (e, b, 0, 0)),
                  pl.BlockSpec((1, 1, cap, 1), lambda e, b: (e, b, 0, 0)),
                  pl.BlockSpec((1, d, ff), lambda e, b: (e, 0, 0)),
                  pl.BlockSpec((1, d, ff), lambda e, b: (e, 0, 0))],
        out_specs=pl.BlockSpec((1, 1, cap, ff), lambda e, b: (e, b, 0, 0)),
        out_shape=jax.ShapeDtypeStruct((ne, batch, cap, ff), BF16),
        compiler_params=_cparams("parallel", "parallel"),
    )(xe, gate, wg, wu)


def _scatter_kernel(pos_ref, hid_ref, wdt_ref, h_ref, o_ref, acc_ref):
    e = pl.program_id(2)
    cap = hid_ref.shape[2]
    seq = h_ref.shape[0]
    tc = min(TOKEN_CHUNK, seq)

    @pl.when(e == 0)
    def _():
        acc_ref[...] = jnp.zeros_like(acc_ref)

    ye_t = _dot_nt(wdt_ref[0], hid_ref[0, 0]).astype(BF16)
    for j in range(seq // tc):
        hit = _one_hot(pos_ref[0, :, j * tc:(j + 1) * tc], cap)
        acc_ref[:, j * tc:(j + 1) * tc] += _dot(ye_t, jnp.where(hit, 1.0, 0.0).astype(BF16))

    @pl.when(e == pl.num_programs(2) - 1)
    def _():
        for j in range(seq // tc):
            o_ref[j * tc:(j + 1) * tc, :] = h_ref[j * tc:(j + 1) * tc, :] + acc_ref[:, j * tc:(j + 1) * tc].T


def moe_scatter(pos3, hid, wd_t, h, batch, td):
    ne, _, cap, ff = hid.shape
    m, d = h.shape
    seq = m // batch
    return pl.pallas_call(
        _scatter_kernel,
        grid=(batch, d // td, ne),
        in_specs=[pl.BlockSpec((1, 1, seq), lambda b, j, e: (e, 0, b)),
                  pl.BlockSpec((1, 1, cap, ff), lambda b, j, e: (e, b, 0, 0)),
                  pl.BlockSpec((1, td, ff), lambda b, j, e: (e, j, 0)),
                  pl.BlockSpec((seq, td), lambda b, j, e: (b, j))],
        out_specs=pl.BlockSpec((seq, td), lambda b, j, e: (b, j)),
        out_shape=jax.ShapeDtypeStruct((m, d), F32),
        scratch_shapes=[pltpu.VMEM((td, seq), F32)],
        compiler_params=_cparams("parallel", "parallel", "arbitrary"),
    )(pos3, hid, wd_t, h)


def _rope_tables(seq):
    half = ROPE_DIM // 2
    inv = ROPE_THETA ** (-(jnp.arange(0, ROPE_DIM, 2, dtype=F32) / ROPE_DIM))
    ang = jnp.arange(seq, dtype=F32)[:, None] * inv[None, :]
    cos, sin = jnp.cos(ang), jnp.sin(ang)
    one = jnp.ones((seq, HEAD - ROPE_DIM), F32)
    zero = lambda n: jnp.zeros((seq, n), F32)
    cos_t = jnp.concatenate([cos, cos, one], axis=1)
    sin_a = jnp.concatenate([-sin, zero(HEAD - half)], axis=1)
    sin_b = jnp.concatenate([zero(half), sin, zero(HEAD - ROPE_DIM)], axis=1)
    return [jnp.tile(t, (1, LANES // HEAD)) for t in (cos_t, sin_a, sin_b)]


def _hi_lo(w):
    hi = w.astype(BF16)
    return hi, (w - hi.astype(F32)).astype(BF16)


def _block_diag2(wf, wb):
    z = jnp.zeros_like(wf)
    return jnp.concatenate([jnp.concatenate([wf, z], axis=1), jnp.concatenate([z, wb], axis=1)], axis=0)


def _col_tile(n):
    for t in (1024, 896, 512, 256, 128):
        if n % t == 0:
            return t
    raise ValueError(f"unsupported matmul width {n}")


def kernel(x, attn_norm_g, w_in, q_norm_g, k_norm_g, lambda_q1, lambda_k1, lambda_q2, lambda_k2, subln_g, shift_mu, w0_f, w2_f, w0_b, w2_b, a0_f, a2_f, a0_b, a2_b, g2, k_k, k_a, r_k, ln_x_w, ln_x_b, w_branch_a, w_branch_b, w_out, ffn_norm_g, w_router, w_gate_e, w_up_e, w_down_e):
    batch, seq, d = x.shape
    m = batch * seq
    depth = w_in.shape[0]
    c = w_branch_b.shape[1]
    qk_w = 2 * w_branch_a.shape[1]
    v_w = w_branch_a.shape[1]
    rw_cols = shift_mu.shape[1]
    rw_pad = -(-rw_cols // LANES) * LANES
    ne = w_router.shape[2]
    cap = 2 * seq // ne
    tm = min(256, seq)
    tmm = 512 if m % 512 == 0 else tm
    cos_t, sin_a, sin_b = _rope_tables(seq)
    row = lambda a: a.reshape(1, -1)

    h = x.reshape(m, d)
    for l in range(depth):
        lam_init = 0.8 - 0.6 * math.exp(-0.3 * l)
        w = w_in[l]
        w_qk = w[:, :qk_w].astype(BF16)
        w_v = w[:, qk_w:qk_w + v_w].astype(BF16)
        w_rw = jnp.pad(w[:, qk_w + v_w:qk_w + v_w + rw_cols], ((0, 0), (0, rw_pad - rw_cols))).astype(BF16)
        w_gt = w[:, qk_w + v_w + rw_cols:].astype(BF16)

        hn = rmsnorm_bf16(h, attn_norm_g[l], tm)
        qk = matmul(hn, w_qk, F32, tmm, _col_tile(qk_w))
        v = matmul(hn, w_v, BF16, tmm, _col_tile(v_w))
        p_rw = matmul(hn, w_rw, F32, tmm, _col_tile(rw_pad))
        gates = matmul(hn, w_gt, BF16, tmm, _col_tile(2 * d), act="sigmoid")

        gq = row(jnp.tile(q_norm_g[l], LANES // HEAD))
        gk = row(jnp.tile(k_norm_g[l], LANES // HEAD))
        q, k = qk_prep(qk, cos_t, sin_a, sin_b, gq, gk, seq, tm)
        lam_params = jnp.stack([lambda_q1[l], lambda_k1[l], lambda_q2[l], lambda_k2[l]])
        y_a = diff_attention(q, k, v, lam_params, subln_g[l], batch, lam_init, tm)

        mu = jnp.pad(shift_mu[l], (0, rw_pad - rw_cols)).reshape(1, rw_pad)
        w0 = row(jnp.concatenate([w0_f[l], w0_b[l]]))
        a0 = row(jnp.concatenate([a0_f[l], a0_b[l]]))
        w2 = _hi_lo(_block_diag2(w2_f[l], w2_b[l]))
        a2 = _hi_lo(_block_diag2(a2_f[l], a2_b[l]))
        g_rows = rw_pad - 3 * c - 2 * LANES
        g2p = _hi_lo(jnp.pad(g2[l], ((0, g_rows - g2.shape[1]), (0, 0))))
        r, vv, kk, k_f, k_b, lw_f, lw_b, a_f, a_b, bonus, g = rwkv_prep(
            p_rw, mu, w0, w2, a0, a2, g2p, row(k_k[l]), row(k_a[l]), row(r_k[l]), seq, tm, c)
        y_f, y_bk = rwkv_scan(r, vv, kk, lw_f, k_f, a_f, lw_b, k_b, a_b, batch)
        y_b = rwkv_post(y_f, y_bk, bonus, g, ln_x_w[l], ln_x_b[l], tm)

        wr_hi, wr_lo = _hi_lo(w_router[l].T)
        h2, hn2, aff_t = merge_out_router(
            y_a, y_b, gates, h, w_branch_a[l].astype(BF16), w_branch_b[l].astype(BF16), w_out[l].astype(BF16),
            row(ffn_norm_g[l]), wr_hi, wr_lo, tm)

        pos = select_slots(aff_t, batch, cap)
        pos3 = pos.reshape(ne, 1, m)
        xe, gate = moe_gather(pos3, aff_t.reshape(ne, 1, m), hn2, batch, cap)
        hid = moe_experts(xe, gate, w_gate_e[l].astype(BF16), w_up_e[l].astype(BF16))
        wd_t = jnp.swapaxes(w_down_e[l], 1, 2).astype(BF16)
        h = moe_scatter(pos3, hid, wd_t, h2, batch, min(512, d))
    return h.reshape(batch, seq, d)
```

```python
import functools
import math

import jax
import jax.numpy as jnp
from jax import lax
from jax.experimental import pallas as pl
from jax.experimental.pallas import tpu as pltpu

F32 = jnp.float32
BF16 = jnp.bfloat16

LANES = 128
HEAD = 64
CHUNK = 64
N_EXPERTS = 16
RMS_EPS = 1e-6
GN_EPS = 64e-5
ROPE_THETA = 500000.0
ROPE_DIM = 16
VMEM_LIMIT = 56 * 1024 * 1024


def _cparams(*sem):
    return pltpu.CompilerParams(dimension_semantics=sem, vmem_limit_bytes=VMEM_LIMIT)


def _dot(a, b):
    return jnp.dot(a, b, preferred_element_type=F32)


def _dot_nt(a, b):
    return lax.dot_general(a, b, (((1,), (1,)), ((), ())), preferred_element_type=F32)


def _dot_tn(a, b):
    return lax.dot_general(a, b, (((0,), (0,)), ((), ())), preferred_element_type=F32)


def _split3(x):
    h1 = x.astype(BF16)
    r1 = x - h1.astype(F32)
    h2 = r1.astype(BF16)
    h3 = (r1 - h2.astype(F32)).astype(BF16)
    return h1, h2, h3


def _dot_exact_lhs(w_bf16, x):
    h1, h2, h3 = _split3(x)
    return _dot(w_bf16, h1) + _dot(w_bf16, h2) + _dot(w_bf16, h3)


def _dot_exact_rhs(x, w_bf16):
    h1, h2, h3 = _split3(x)
    return _dot(h1, w_bf16) + _dot(h2, w_bf16) + _dot(h3, w_bf16)


def _scan_chains(r_ref, v_ref, kk_ref, lw_ref, k_ref, a_ref, y_ref, d, reverse):
    L = CHUNK
    n_pairs = r_ref.shape[-1] // LANES
    t_i = lax.broadcasted_iota(jnp.int32, (L, L), 0)
    i_i = lax.broadcasted_iota(jnp.int32, (L, L), 1)
    tri = jnp.where((i_i >= t_i) if reverse else (i_i <= t_i), 1.0, 0.0).astype(BF16)

    lw = lw_ref[...]
    cum = _dot_exact_lhs(tri, lw)
    last = 0 if reverse else L - 1
    tot = cum[last:last + 1, :]
    winv = jnp.exp(-cum)
    wrem = jnp.exp(tot - cum)
    kk = kk_ref[...]
    b = kk * a_ref[...]
    k = k_ref[...]
    rt = r_ref[...] * jnp.exp(cum)
    at = -kk * jnp.exp(cum - lw)
    bt = (b * winv).astype(BF16)
    kt = (k * winv).astype(BF16)
    bh = (b * wrem).astype(BF16)
    kh = (k * wrem).astype(BF16)
    w_tot = jnp.exp(tot)
    v = v_ref[...].astype(BF16)

    lane = lax.broadcasted_iota(jnp.int32, (1, LANES), 1)
    lo = lane < HEAD
    t2 = lax.broadcasted_iota(jnp.int32, (L, LANES), 0)
    i2 = lax.broadcasted_iota(jnp.int32, (L, LANES), 1) % HEAD
    strict = (i2 > t2) if reverse else (i2 < t2)
    incl = (i2 >= t2) if reverse else (i2 <= t2)
    chains = []
    for p in range(n_pairs):
        sl = slice(p * LANES, (p + 1) * LANES)
        at_p, rt_p = at[:, sl], rt[:, sl]
        chains.append(dict(
            d=d, p=p, sl=sl, y_ref=y_ref, strict=strict, incl=incl,
            a_lo=jnp.where(lo, at_p, 0.0).astype(BF16), a_hi=jnp.where(lo, 0.0, at_p).astype(BF16),
            r_lo=jnp.where(lo, rt_p, 0.0).astype(BF16), r_hi=jnp.where(lo, 0.0, rt_p).astype(BF16),
            bt=bt[:, sl], kt=kt[:, sl], v=v[:, sl], bh=bh[:, sl], kh=kh[:, sl], w_tot=w_tot[:, sl]))
    return chains


def _rwkv_scan_kernel(r_f, v_f, kk_f, lw_f, k_f, a_f, r_b, v_b, kk_b, lw_b, k_b, a_b, yf_ref, yb_ref, s_ref):
    @pl.when(pl.program_id(1) == 0)
    def _():
        s_ref[...] = jnp.zeros_like(s_ref)

    L = CHUNK
    cs = (_scan_chains(r_f, v_f, kk_f, lw_f, k_f, a_f, yf_ref, 0, False)
          + _scan_chains(r_b, v_b, kk_b, lw_b, k_b, a_b, yb_ref, 1, True))
    lo = lax.broadcasted_iota(jnp.int32, (1, LANES), 1) < HEAD
    rr = lax.broadcasted_iota(jnp.int32, (LANES, LANES), 0) < HEAD
    cc = lax.broadcasted_iota(jnp.int32, (LANES, LANES), 1) < HEAD
    same_head = rr == cc
    zeros = jnp.zeros((L, LANES), BF16)
    cat = lambda *xs: jnp.concatenate(xs, axis=0)

    g0 = [_dot_nt(cat(c["a_lo"], c["r_lo"]), cat(c["bt"], c["kt"])) for c in cs]
    g1 = [_dot_nt(cat(c["a_hi"], c["r_hi"]), cat(c["kt"], c["bt"])) for c in cs]
    m0 = [jnp.where(c["strict"], g[:L], 0.0) for c, g in zip(cs, g0)]
    n0 = [jnp.where(c["incl"], g[L:], 0.0).astype(BF16) for c, g in zip(cs, g0)]
    m1 = [jnp.where(c["strict"], g[:L], 0.0) for c, g in zip(cs, g1)]
    n1 = [jnp.where(c["incl"], g[L:], 0.0).astype(BF16) for c, g in zip(cs, g1)]
    s = [s_ref[c["d"], c["p"]] for c in cs]
    ars = [_dot_nt(cat(c["a_lo"], c["a_hi"], c["r_lo"], c["r_hi"]), si.astype(BF16)) for c, si in zip(cs, s)]
    akv0 = [_dot(m.astype(BF16), cat(zeros, c["v"])) for c, m in zip(cs, m0)]
    akv1 = [_dot(m.astype(BF16), cat(c["v"], zeros)) for c, m in zip(cs, m1)]
    x = [a[:2 * L] + cat(k0, k1) for a, k0, k1 in zip(ars, akv0, akv1)]
    pw = [cat(jnp.where(lo, a, 0.0), jnp.where(lo, 0.0, b)).astype(BF16) for a, b in zip(m0, m1)]
    n_sq = int(math.log2(L))
    for j in range(n_sq):
        x = [xi + _dot(pi, xi.astype(BF16)) for xi, pi in zip(x, pw)]
        if j + 1 < n_sq:
            pw = [_dot(pi, pi).astype(BF16) for pi in pw]
    y0 = [a[2 * L:3 * L] + _dot(n, cat(xi[:L].astype(BF16), c["v"])) for c, a, n, xi in zip(cs, ars, n0, x)]
    y1 = [a[3 * L:] + _dot(n, cat(c["v"], xi[L:].astype(BF16))) for c, a, n, xi in zip(cs, ars, n1, x)]
    for c, a, b in zip(cs, y0, y1):
        c["y_ref"][:, c["sl"]] = jnp.where(lo, a, b)
    ds = [_dot_tn(cat(jnp.where(lo, xi[:L], xi[L:]).astype(BF16), c["v"]), cat(c["bh"], c["kh"]))
          for c, xi in zip(cs, x)]
    for c, si, di in zip(cs, s, ds):
        s_ref[c["d"], c["p"]] = si * c["w_tot"] + jnp.where(same_head, di, 0.0)


def rwkv_scan(r, v, kk, lw_f, k_f, a_f, lw_b, k_b, a_b, batch):
    m, c = r.shape
    nc = m // batch // CHUNK
    fwd = pl.BlockSpec((CHUNK, c), lambda bi, ci: (bi * nc + ci, 0))
    bwd = pl.BlockSpec((CHUNK, c), lambda bi, ci: (bi * nc + nc - 1 - ci, 0))
    return pl.pallas_call(
        _rwkv_scan_kernel,
        grid=(batch, nc),
        in_specs=[fwd] * 6 + [bwd] * 6,
        out_specs=[fwd, bwd],
        out_shape=[jax.ShapeDtypeStruct((m, c), F32)] * 2,
        scratch_shapes=[pltpu.VMEM((2, c // LANES, LANES, LANES), F32)],
        compiler_params=_cparams("arbitrary", "arbitrary"),
    )(r, v, kk, lw_f, k_f, a_f, r, v, kk, lw_b, k_b, a_b)


def _sigmoid(x):
    return 1.0 / (1.0 + jnp.exp(-x))


def _split2(x):
    hi = x.astype(BF16)
    return hi, (x - hi.astype(F32)).astype(BF16)


def _dot_f32(a, w_hi, w_lo):
    a_hi, a_lo = _split2(a)
    return _dot(a_hi, w_hi) + _dot(a_lo, w_hi) + _dot(a_hi, w_lo)


def _head_block_ones():
    r = lax.broadcasted_iota(jnp.int32, (LANES, LANES), 0) // HEAD
    c = lax.broadcasted_iota(jnp.int32, (LANES, LANES), 1) // HEAD
    return jnp.where(r == c, 1.0, 0.0).astype(BF16)


def _group_sum(x, ones):
    tiles = [_dot_exact_rhs(x[:, j * LANES:(j + 1) * LANES], ones) for j in range(x.shape[1] // LANES)]
    return tiles[0] if len(tiles) == 1 else jnp.concatenate(tiles, axis=1)


def _rmsnorm_kernel(x_ref, g_ref, o_ref):
    x = x_ref[...]
    y = x * lax.rsqrt(jnp.mean(x * x, axis=-1, keepdims=True) + RMS_EPS)
    o_ref[...] = (y * g_ref[...]).astype(o_ref.dtype)


def rmsnorm_bf16(x, g, tm):
    m, d = x.shape
    return pl.pallas_call(
        _rmsnorm_kernel,
        grid=(m // tm,),
        in_specs=[pl.BlockSpec((tm, d), lambda i: (i, 0)), pl.BlockSpec((1, d), lambda i: (0, 0))],
        out_specs=pl.BlockSpec((tm, d), lambda i: (i, 0)),
        out_shape=jax.ShapeDtypeStruct((m, d), BF16),
        compiler_params=_cparams("parallel"),
    )(x, g.reshape(1, d))


def _matmul_kernel(a_ref, w_ref, o_ref, *, act):
    acc = _dot(a_ref[...], w_ref[...])
    if act == "sigmoid":
        acc = _sigmoid(acc)
    o_ref[...] = acc.astype(o_ref.dtype)


def matmul(a, w, out_dtype, tm, tn, act=None):
    m, k = a.shape
    n = w.shape[1]
    return pl.pallas_call(
        functools.partial(_matmul_kernel, act=act),
        grid=(n // tn, m // tm),
        in_specs=[pl.BlockSpec((tm, k), lambda j, i: (i, 0)), pl.BlockSpec((k, tn), lambda j, i: (0, j))],
        out_specs=pl.BlockSpec((tm, tn), lambda j, i: (i, j)),
        out_shape=jax.ShapeDtypeStruct((m, n), out_dtype),
        compiler_params=_cparams("parallel", "parallel"),
    )(a, w)


def _qk_prep_kernel(qk_ref, cos_ref, sin_a_ref, sin_b_ref, gq_ref, gk_ref, q_ref, k_ref):
    ones = _head_block_ones()
    cos, sin_a, sin_b = cos_ref[...], sin_a_ref[...], sin_b_ref[...]
    n_tiles = q_ref.shape[1] // LANES
    for j in range(2 * n_tiles):
        x = qk_ref[:, j * LANES:(j + 1) * LANES]
        ms = _group_sum(x * x, ones) * (1.0 / HEAD)
        is_q = j < n_tiles
        y = x * lax.rsqrt(ms + RMS_EPS) * (gq_ref[...] if is_q else gk_ref[...])
        y = y * cos + pltpu.roll(y, LANES - ROPE_DIM // 2, 1) * sin_a + pltpu.roll(y, ROPE_DIM // 2, 1) * sin_b
        if is_q:
            q_ref[:, j * LANES:(j + 1) * LANES] = (y * HEAD ** -0.5).astype(q_ref.dtype)
        else:
            jj = j - n_tiles
            k_ref[:, jj * LANES:(jj + 1) * LANES] = y.astype(k_ref.dtype)


def qk_prep(qk, cos, sin_a, sin_b, gq, gk, seq, tm):
    m, w2 = qk.shape
    w = w2 // 2
    st = seq // tm
    tab = pl.BlockSpec((tm, LANES), lambda i: (i % st, 0))
    vec = pl.BlockSpec((1, LANES), lambda i: (0, 0))
    return pl.pallas_call(
        _qk_prep_kernel,
        grid=(m // tm,),
        in_specs=[pl.BlockSpec((tm, w2), lambda i: (i, 0)), tab, tab, tab, vec, vec],
        out_specs=[pl.BlockSpec((tm, w), lambda i: (i, 0))] * 2,
        out_shape=[jax.ShapeDtypeStruct((m, w), BF16)] * 2,
        compiler_params=_cparams("parallel"),
    )(qk, cos, sin_a, sin_b, gq, gk)


KEY_CHUNK = 512
MAX_SCORE_BOUND = 40.0


def _attn_kernel(bound_ref, lam_ref, q_ref, k_ref, v_ref, g_ref, o_ref, vaug_ref, *, lam_init):
    lp = lam_ref[...]
    lam = (jnp.exp(jnp.sum(lp[0:1] * lp[1:2], axis=-1, keepdims=True))
           - jnp.exp(jnp.sum(lp[2:3] * lp[3:4], axis=-1, keepdims=True)) + lam_init)
    seq = k_ref.shape[0]
    tq = q_ref.shape[0]
    tk = min(KEY_CHUNK, seq)

    @pl.when(pl.program_id(2) == 0)
    def _():
        vaug_ref[:, :LANES] = v_ref[...]
        vaug_ref[:, LANES:] = jnp.ones((seq, LANES), BF16)

    q = q_ref[...]
    lo = lax.broadcasted_iota(jnp.int32, (1, LANES), 1) < HEAD
    zero = jnp.zeros_like(q)
    q_lo, q_hi = jnp.where(lo, q, zero), jnp.where(lo, zero, q)

    def finish(o2):
        o = o2[:tq] - lam * o2[tq:]
        o = o * lax.rsqrt(jnp.mean(o * o, axis=-1, keepdims=True) + RMS_EPS)
        o_ref[...] = (o * g_ref[...] * (1.0 - lam_init)).astype(o_ref.dtype)

    bound = bound_ref[0, 0]
    safe = bound <= MAX_SCORE_BOUND

    @pl.when(safe)
    def _():
        q2 = jnp.concatenate([q_lo, q_hi], axis=0)
        acc = jnp.zeros((2 * tq, 2 * LANES), F32)
        for c in range(seq // tk):
            s = _dot_nt(q2, k_ref[c * tk:(c + 1) * tk, :])
            acc = acc + _dot(jnp.exp(s - bound).astype(BF16), vaug_ref[c * tk:(c + 1) * tk, :])
        finish(acc[:, :LANES] / acc[:, LANES:LANES + 1])

    @pl.when(jnp.logical_not(safe))
    def _():
        def branch(qm):
            s = _dot_nt(qm, k_ref[...])
            p = jnp.exp(s - jnp.max(s, axis=-1, keepdims=True))
            return _dot(p.astype(BF16), v_ref[...]) / jnp.sum(p, axis=-1, keepdims=True)

        finish(jnp.concatenate([branch(q_lo), branch(q_hi)], axis=0))


def diff_attention(q, k, v, score_bound, lam_params, subln_g, batch, lam_init, tq):
    m, w = q.shape
    seq = m // batch
    nq = seq // tq
    return pl.pallas_call(
        functools.partial(_attn_kernel, lam_init=lam_init),
        grid=(batch, w // LANES, nq),
        in_specs=[pl.BlockSpec(memory_space=pltpu.SMEM),
                  pl.BlockSpec((4, HEAD), lambda b, h, i: (0, 0)),
                  pl.BlockSpec((tq, LANES), lambda b, h, i: (b * nq + i, h)),
                  pl.BlockSpec((seq, LANES), lambda b, h, i: (b, h)),
                  pl.BlockSpec((seq, LANES), lambda b, h, i: (b, h)),
                  pl.BlockSpec((1, LANES), lambda b, h, i: (0, 0))],
        out_specs=pl.BlockSpec((tq, LANES), lambda b, h, i: (b * nq + i, h)),
        out_shape=jax.ShapeDtypeStruct((m, w), BF16),
        scratch_shapes=[pltpu.VMEM((seq, 2 * LANES), BF16)],
        compiler_params=_cparams("parallel", "parallel", "arbitrary"),
    )(score_bound, lam_params, q, k, v, subln_g.reshape(1, LANES))


def _rwkv_prep_kernel(p_ref, prev_ref, next_ref, mu_ref, w0_ref, w2h_ref, w2l_ref, a0_ref, a2h_ref, a2l_ref,
                      g2h_ref, g2l_ref, kkw_ref, ka_ref, rk_ref,
                      r_o, v_o, kk_o, kf_o, kb_o, lwf_o, lwb_o, af_o, ab_o, bonus_o, g_o, *, seq_tiles):
    tm = p_ref.shape[0]
    c = r_o.shape[1]
    ti = pl.program_id(0) % seq_tiles
    p = p_ref[...]
    row = lax.broadcasted_iota(jnp.int32, (tm, 1), 0)
    prev_row = jnp.where(ti > 0, prev_ref[7:8, :], 0.0)
    next_row = jnp.where(ti < seq_tiles - 1, next_ref[0:1, :], 0.0)
    prev = jnp.where(row == 0, prev_row, pltpu.roll(p, 1, 0))
    nxt = jnp.where(row == tm - 1, next_row, pltpu.roll(p, tm - 1, 0))
    p = p + mu_ref[...] * (0.5 * (prev + nxt) - p)

    r, k, v = p[:, :c], p[:, c:2 * c], p[:, 2 * c:3 * c]
    wd = p[:, 3 * c:3 * c + LANES]
    ad = p[:, 3 * c + LANES:3 * c + 2 * LANES]
    gd = p[:, 3 * c + 2 * LANES:]
    wl = w0_ref[...] + _dot_f32(jnp.tanh(wd), w2h_ref[...], w2l_ref[...])
    lw = -math.exp(-0.5) * _sigmoid(wl)
    a = _sigmoid(a0_ref[...] + _dot_f32(ad, a2h_ref[...], a2l_ref[...]))
    g_o[...] = _dot_f32(_sigmoid(gd), g2h_ref[...], g2l_ref[...])
    a_f, a_b = a[:, :c], a[:, c:]
    ones = _head_block_ones()
    kk = k * kkw_ref[...]
    kk = kk / jnp.maximum(jnp.sqrt(_group_sum(kk * kk, ones)), 1e-12)
    k_a = ka_ref[...]
    k_f = k * (1.0 + (a_f - 1.0) * k_a)
    k_b = k * (1.0 + (a_b - 1.0) * k_a)
    bonus_o[...] = _group_sum(r * (k_f + k_b) * rk_ref[...], ones) * v
    r_o[...] = r
    v_o[...] = v
    kk_o[...] = kk
    kf_o[...] = k_f
    kb_o[...] = k_b
    lwf_o[...] = lw[:, :c]
    lwb_o[...] = lw[:, c:]
    af_o[...] = a_f
    ab_o[...] = a_b


def rwkv_prep(p, mu, w0, w2, a0, a2, g2, k_k, k_a, r_k, seq, tm, c):
    m, pc = p.shape
    st = seq // tm
    hb = tm // 8
    last = m // 8 - 1
    full = lambda a: pl.BlockSpec(a.shape, lambda i: (0, 0))
    row_c = pl.BlockSpec((tm, c), lambda i: (i, 0))
    small = [mu, w0, w2[0], w2[1], a0, a2[0], a2[1], g2[0], g2[1], k_k, k_a, r_k]
    return pl.pallas_call(
        functools.partial(_rwkv_prep_kernel, seq_tiles=st),
        grid=(m // tm,),
        in_specs=[pl.BlockSpec((tm, pc), lambda i: (i, 0)),
                  pl.BlockSpec((8, pc), lambda i: (jnp.maximum(i * hb - 1, 0), 0)),
                  pl.BlockSpec((8, pc), lambda i: (jnp.minimum((i + 1) * hb, last), 0))] + [full(a) for a in small],
        out_specs=[row_c] * 11,
        out_shape=[jax.ShapeDtypeStruct((m, c), F32)] * 11,
        compiler_params=_cparams("parallel"),
    )(p, p, p, *small)


def _rwkv_post_kernel(yf_ref, yb_ref, bonus_ref, g_ref, lnw_ref, lnb_ref, o_ref):
    ones = _head_block_ones()
    y = yf_ref[...] + yb_ref[...]
    mean = _group_sum(y, ones) * (1.0 / HEAD)
    yc = y - mean
    var = _group_sum(yc * yc, ones) * (1.0 / HEAD)
    yn = yc * lax.rsqrt(var + GN_EPS) * lnw_ref[...] + lnb_ref[...]
    o_ref[...] = ((yn + bonus_ref[...]) * g_ref[...]).astype(o_ref.dtype)


def rwkv_post(yf, yb, bonus, g, ln_w, ln_b, tm):
    m, c = yf.shape
    row = pl.BlockSpec((tm, c), lambda i: (i, 0))
    vec = pl.BlockSpec((1, c), lambda i: (0, 0))
    return pl.pallas_call(
        _rwkv_post_kernel,
        grid=(m // tm,),
        in_specs=[row] * 4 + [vec] * 2,
        out_specs=row,
        out_shape=jax.ShapeDtypeStruct((m, c), BF16),
        compiler_params=_cparams("parallel"),
    )(yf, yb, bonus, g, ln_w.reshape(1, c), ln_b.reshape(1, c))


def _merge_kernel(ya_ref, yb_ref, gate_ref, x_ref, wa_ref, wb_ref, wo_ref, fg_ref, wrh_ref, wrl_ref,
                  h_ref, hn_ref, aff_ref):
    d = x_ref.shape[1]
    merged = (gate_ref[:, :d].astype(F32) * _dot(ya_ref[...], wa_ref[...])
              + gate_ref[:, d:].astype(F32) * _dot(yb_ref[...], wb_ref[...]))
    h = x_ref[...] + _dot(merged.astype(BF16), wo_ref[...])
    h_ref[...] = h
    hn = h * lax.rsqrt(jnp.mean(h * h, axis=-1, keepdims=True) + RMS_EPS) * fg_ref[...]
    hn_ref[...] = hn.astype(hn_ref.dtype)
    hn_hi, hn_lo = _split2(hn)
    wr_hi = wrh_ref[...]
    logits = _dot_nt(wr_hi, hn_hi) + _dot_nt(wr_hi, hn_lo) + _dot_nt(wrl_ref[...], hn_hi)
    e = jnp.exp(logits - jnp.max(logits, axis=0, keepdims=True))
    aff_ref[...] = e / jnp.sum(e, axis=0, keepdims=True)


def merge_out_router(ya, yb, gates, x, wa, wb, wo, ffn_g, wr_hi, wr_lo, tm):
    m, d = x.shape
    full = lambda a: pl.BlockSpec(a.shape, lambda i: (0, 0))
    row = lambda a: pl.BlockSpec((tm, a.shape[1]), lambda i: (i, 0))
    ne = wr_hi.shape[0]
    return pl.pallas_call(
        _merge_kernel,
        grid=(m // tm,),
        in_specs=[row(ya), row(yb), row(gates), row(x), full(wa), full(wb), full(wo), full(ffn_g),
                  full(wr_hi), full(wr_lo)],
        out_specs=[pl.BlockSpec((tm, d), lambda i: (i, 0)), pl.BlockSpec((tm, d), lambda i: (i, 0)),
                   pl.BlockSpec((ne, tm), lambda i: (0, i))],
        out_shape=[jax.ShapeDtypeStruct((m, d), F32), jax.ShapeDtypeStruct((m, d), BF16),
                   jax.ShapeDtypeStruct((ne, m), F32)],
        compiler_params=_cparams("parallel"),
    )(ya, yb, gates, x, wa, wb, wo, ffn_g, wr_hi, wr_lo)


def _prefix_excl(m_bf16, upper):
    rows, t = m_bf16.shape
    off = jnp.zeros((rows, 1), F32)
    out = []
    for j in range(t // LANES):
        blk = m_bf16[:, j * LANES:(j + 1) * LANES]
        out.append(_dot(blk, upper) + off)
        off = off + jnp.sum(blk.astype(F32), axis=-1, keepdims=True)
    return jnp.concatenate(out, axis=1)


def _select_kernel(aff_ref, pos_ref, *, cap):
    aff = aff_ref[...]
    count = lambda pred: jnp.sum(jnp.where(pred, 1.0, 0.0), axis=-1, keepdims=True)
    as_f32 = lambda bits: pltpu.bitcast(jnp.broadcast_to(bits, aff.shape), F32)

    def step(i, thr):
        cand = thr | (jnp.int32(1) << (30 - i))
        return jnp.where(count(aff >= as_f32(cand)) >= cap, cand, thr)

    thr = lax.fori_loop(0, 31, step, jnp.zeros((aff.shape[0], 1), jnp.int32))
    gt = aff >= as_f32(thr + 1)
    eq = (aff >= as_f32(thr)) & jnp.logical_not(gt)
    r = lax.broadcasted_iota(jnp.int32, (LANES, LANES), 0)
    c = lax.broadcasted_iota(jnp.int32, (LANES, LANES), 1)
    upper = jnp.where(r < c, 1.0, 0.0).astype(BF16)
    need = cap - jnp.sum(jnp.where(gt, 1.0, 0.0), axis=-1, keepdims=True)
    eq_rank = _prefix_excl(jnp.where(eq, 1.0, 0.0).astype(BF16), upper)
    sel = gt | (eq & (eq_rank < need))
    pos = _prefix_excl(jnp.where(sel, 1.0, 0.0).astype(BF16), upper)
    pos_ref[...] = jnp.where(sel, pos, -1.0).astype(jnp.int32)


def select_slots(aff_t, batch, cap):
    ne, m = aff_t.shape
    seq = m // batch
    return pl.pallas_call(
        functools.partial(_select_kernel, cap=cap),
        grid=(batch,),
        in_specs=[pl.BlockSpec((ne, seq), lambda b: (0, b))],
        out_specs=pl.BlockSpec((ne, seq), lambda b: (0, b)),
        out_shape=jax.ShapeDtypeStruct((ne, m), jnp.int32),
        compiler_params=_cparams("parallel"),
    )(aff_t)


TOKEN_CHUNK = 1024


def _one_hot(pos_chunk, cap):
    slot = lax.broadcasted_iota(jnp.int32, (cap, pos_chunk.shape[1]), 0)
    return pos_chunk == slot


def _gather_kernel(pos_ref, aff_ref, hn_ref, xe_ref, gate_ref):
    cap = xe_ref.shape[2]
    seq = hn_ref.shape[0]
    tc = min(TOKEN_CHUNK, seq)
    xe = jnp.zeros(xe_ref.shape[2:], F32)
    gate = jnp.zeros((cap, 1), F32)
    for j in range(seq // tc):
        hit = _one_hot(pos_ref[0, :, j * tc:(j + 1) * tc], cap)
        xe = xe + _dot(jnp.where(hit, 1.0, 0.0).astype(BF16), hn_ref[j * tc:(j + 1) * tc, :])
        gate = gate + jnp.sum(jnp.where(hit, aff_ref[0, :, j * tc:(j + 1) * tc], 0.0), axis=-1, keepdims=True)
    xe_ref[0, 0] = xe.astype(xe_ref.dtype)
    gate_ref[0, 0] = gate


def moe_gather(pos3, aff3, hn, batch, cap):
    ne = pos3.shape[0]
    m, d = hn.shape
    seq = m // batch
    row = pl.BlockSpec((1, 1, seq), lambda b, e: (e, 0, b))
    return pl.pallas_call(
        _gather_kernel,
        grid=(batch, ne),
        in_specs=[row, row, pl.BlockSpec((seq, d), lambda b, e: (b, 0))],
        out_specs=[pl.BlockSpec((1, 1, cap, d), lambda b, e: (e, b, 0, 0)),
                   pl.BlockSpec((1, 1, cap, 1), lambda b, e: (e, b, 0, 0))],
        out_shape=[jax.ShapeDtypeStruct((ne, batch, cap, d), BF16),
                   jax.ShapeDtypeStruct((ne, batch, cap, 1), F32)],
        compiler_params=_cparams("parallel", "parallel"),
    )(pos3, aff3, hn)


def _expert_kernel(xe_ref, gate_ref, wg_ref, wu_ref, hid_ref, wg_bf, wu_bf):
    @pl.when(pl.program_id(1) == 0)
    def _():
        wg_bf[...] = wg_ref[0].astype(BF16)
        wu_bf[...] = wu_ref[0].astype(BF16)

    xe = xe_ref[0, 0]
    hg = _dot(xe, wg_bf[...])
    hu = _dot(xe, wu_bf[...])
    hid_ref[0, 0] = (hg * _sigmoid(hg) * hu * gate_ref[0, 0]).astype(hid_ref.dtype)


def moe_experts(xe, gate, wg, wu):
    ne, batch, cap, d = xe.shape
    ff = wg.shape[2]
    return pl.pallas_call(
        _expert_kernel,
        grid=(ne, batch),
        in_specs=[pl.BlockSpec((1, 1, cap, d), lambda e, b: (e, b, 0, 0)),
                  pl.BlockSpec((1, 1, cap, 1), lambda e, b: (e, b, 0, 0)),
                  pl.BlockSpec((1, d, ff), lambda e, b: (e, 0, 0)),
                  pl.BlockSpec((1, d, ff), lambda e, b: (e, 0, 0))],
        out_specs=pl.BlockSpec((1, 1, cap, ff), lambda e, b: (e, b, 0, 0)),
        out_shape=jax.ShapeDtypeStruct((ne, batch, cap, ff), BF16),
        scratch_shapes=[pltpu.VMEM((d, ff), BF16)] * 2,
        compiler_params=_cparams("parallel", "arbitrary"),
    )(xe, gate, wg, wu)


def _scatter_kernel(pos_ref, hid_ref, wdt_ref, h_ref, o_ref, acc_ref):
    e = pl.program_id(2)
    cap = hid_ref.shape[2]
    seq = h_ref.shape[0]
    tc = min(TOKEN_CHUNK, seq)

    @pl.when(e == 0)
    def _():
        acc_ref[...] = jnp.zeros_like(acc_ref)

    ye_t = _dot_nt(wdt_ref[0], hid_ref[0, 0]).astype(BF16)
    for j in range(seq // tc):
        hit = _one_hot(pos_ref[0, :, j * tc:(j + 1) * tc], cap)
        acc_ref[:, j * tc:(j + 1) * tc] += _dot(ye_t, jnp.where(hit, 1.0, 0.0).astype(BF16))

    @pl.when(e == pl.num_programs(2) - 1)
    def _():
        for j in range(seq // tc):
            o_ref[j * tc:(j + 1) * tc, :] = h_ref[j * tc:(j + 1) * tc, :] + acc_ref[:, j * tc:(j + 1) * tc].T


def moe_scatter(pos3, hid, wd_t, h, batch, td):
    ne, _, cap, ff = hid.shape
    m, d = h.shape
    seq = m // batch
    return pl.pallas_call(
        _scatter_kernel,
        grid=(batch, d // td, ne),
        in_specs=[pl.BlockSpec((1, 1, seq), lambda b, j, e: (e, 0, b)),
                  pl.BlockSpec((1, 1, cap, ff), lambda b, j, e: (e, b, 0, 0)),
                  pl.BlockSpec((1, td, ff), lambda b, j, e: (e, j, 0)),
                  pl.BlockSpec((seq, td), lambda b, j, e: (b, j))],
        out_specs=pl.BlockSpec((seq, td), lambda b, j, e: (b, j)),
        out_shape=jax.ShapeDtypeStruct((m, d), F32),
        scratch_shapes=[pltpu.VMEM((td, seq), F32)],
        compiler_params=_cparams("parallel", "parallel", "arbitrary"),
    )(pos3, hid, wd_t, h)


def _rope_tables(seq):
    half = ROPE_DIM // 2
    inv = ROPE_THETA ** (-(jnp.arange(0, ROPE_DIM, 2, dtype=F32) / ROPE_DIM))
    ang = jnp.arange(seq, dtype=F32)[:, None] * inv[None, :]
    cos, sin = jnp.cos(ang), jnp.sin(ang)
    one = jnp.ones((seq, HEAD - ROPE_DIM), F32)
    zero = lambda n: jnp.zeros((seq, n), F32)
    cos_t = jnp.concatenate([cos, cos, one], axis=1)
    sin_a = jnp.concatenate([-sin, zero(HEAD - half)], axis=1)
    sin_b = jnp.concatenate([zero(half), sin, zero(HEAD - ROPE_DIM)], axis=1)
    return [jnp.tile(t, (1, LANES // HEAD)) for t in (cos_t, sin_a, sin_b)]


def _hi_lo(w):
    hi = w.astype(BF16)
    return hi, (w - hi.astype(F32)).astype(BF16)


def _block_diag2(wf, wb):
    z = jnp.zeros_like(wf)
    return jnp.concatenate([jnp.concatenate([wf, z], axis=1), jnp.concatenate([z, wb], axis=1)], axis=0)


def _col_tile(n):
    for t in (1024, 896, 512, 256, 128):
        if n % t == 0:
            return t
    raise ValueError(f"unsupported matmul width {n}")


def kernel(x, attn_norm_g, w_in, q_norm_g, k_norm_g, lambda_q1, lambda_k1, lambda_q2, lambda_k2, subln_g, shift_mu, w0_f, w2_f, w0_b, w2_b, a0_f, a2_f, a0_b, a2_b, g2, k_k, k_a, r_k, ln_x_w, ln_x_b, w_branch_a, w_branch_b, w_out, ffn_norm_g, w_router, w_gate_e, w_up_e, w_down_e):
    batch, seq, d = x.shape
    m = batch * seq
    depth = w_in.shape[0]
    c = w_branch_b.shape[1]
    qk_w = 2 * w_branch_a.shape[1]
    v_w = w_branch_a.shape[1]
    rw_cols = shift_mu.shape[1]
    rw_pad = -(-rw_cols // LANES) * LANES
    ne = w_router.shape[2]
    cap = 2 * seq // ne
    tm = min(256, seq)
    tmm = 512 if m % 512 == 0 else tm
    cos_t, sin_a, sin_b = _rope_tables(seq)
    row = lambda a: a.reshape(1, -1)

    h = x.reshape(m, d)
    for l in range(depth):
        lam_init = 0.8 - 0.6 * math.exp(-0.3 * l)
        w = w_in[l]
        w_qk = w[:, :qk_w].astype(BF16)
        w_v = w[:, qk_w:qk_w + v_w].astype(BF16)
        w_rw = jnp.pad(w[:, qk_w + v_w:qk_w + v_w + rw_cols], ((0, 0), (0, rw_pad - rw_cols))).astype(BF16)
        w_gt = w[:, qk_w + v_w + rw_cols:].astype(BF16)

        hn = rmsnorm_bf16(h, attn_norm_g[l], tm)
        qk = matmul(hn, w_qk, F32, tmm, _col_tile(qk_w))
        v = matmul(hn, w_v, BF16, tmm, _col_tile(v_w))
        p_rw = matmul(hn, w_rw, F32, tmm, _col_tile(rw_pad))
        gates = matmul(hn, w_gt, BF16, tmm, _col_tile(2 * d), act="sigmoid")

        gq = row(jnp.tile(q_norm_g[l], LANES // HEAD))
        gk = row(jnp.tile(k_norm_g[l], LANES // HEAD))
        q, k = qk_prep(qk, cos_t, sin_a, sin_b, gq, gk, seq, tm)
        lam_params = jnp.stack([lambda_q1[l], lambda_k1[l], lambda_q2[l], lambda_k2[l]])
        score_bound = (1.02 * math.sqrt(HEAD) * jnp.max(jnp.abs(q_norm_g[l])) * jnp.max(jnp.abs(k_norm_g[l]))
                       ).astype(F32).reshape(1, 1)
        y_a = diff_attention(q, k, v, score_bound, lam_params, subln_g[l], batch, lam_init, min(512, seq))

        mu = jnp.pad(shift_mu[l], (0, rw_pad - rw_cols)).reshape(1, rw_pad)
        w0 = row(jnp.concatenate([w0_f[l], w0_b[l]]))
        a0 = row(jnp.concatenate([a0_f[l], a0_b[l]]))
        w2 = _hi_lo(_block_diag2(w2_f[l], w2_b[l]))
        a2 = _hi_lo(_block_diag2(a2_f[l], a2_b[l]))
        g_rows = rw_pad - 3 * c - 2 * LANES
        g2p = _hi_lo(jnp.pad(g2[l], ((0, g_rows - g2.shape[1]), (0, 0))))
        r, vv, kk, k_f, k_b, lw_f, lw_b, a_f, a_b, bonus, g = rwkv_prep(
            p_rw, mu, w0, w2, a0, a2, g2p, row(k_k[l]), row(k_a[l]), row(r_k[l]), seq, tm, c)
        y_f, y_bk = rwkv_scan(r, vv, kk, lw_f, k_f, a_f, lw_b, k_b, a_b, batch)
        y_b = rwkv_post(y_f, y_bk, bonus, g, ln_x_w[l], ln_x_b[l], tm)

        wr_hi, wr_lo = _hi_lo(w_router[l].T)
        h2, hn2, aff_t = merge_out_router(
            y_a, y_b, gates, h, w_branch_a[l].astype(BF16), w_branch_b[l].astype(BF16), w_out[l].astype(BF16),
            row(ffn_norm_g[l]), wr_hi, wr_lo, tm)

        pos = select_slots(aff_t, batch, cap)
        pos3 = pos.reshape(ne, 1, m)
        xe, gate = moe_gather(pos3, aff_t.reshape(ne, 1, m), hn2, batch, cap)
        hid = moe_experts(xe, gate, w_gate_e[l], w_up_e[l])
        wd_t = jnp.swapaxes(w_down_e[l], 1, 2).astype(BF16)
        h = moe_scatter(pos3, hid, wd_t, h2, batch, min(512, d))
    return h.reshape(batch, seq, d)
```

```python
import functools
import math

import jax
import jax.numpy as jnp
from jax import lax
from jax.experimental import pallas as pl
from jax.experimental.pallas import tpu as pltpu

F32 = jnp.float32
BF16 = jnp.bfloat16

LANES = 128
HEAD = 64
CHUNK = 64
N_EXPERTS = 16
RMS_EPS = 1e-6
GN_EPS = 64e-5
ROPE_THETA = 500000.0
ROPE_DIM = 16
VMEM_LIMIT = 56 * 1024 * 1024


def _cparams(*sem):
    return pltpu.CompilerParams(dimension_semantics=sem, vmem_limit_bytes=VMEM_LIMIT)


def _dot(a, b):
    return jnp.dot(a, b, preferred_element_type=F32)


def _dot_nt(a, b):
    return lax.dot_general(a, b, (((1,), (1,)), ((), ())), preferred_element_type=F32)


def _dot_tn(a, b):
    return lax.dot_general(a, b, (((0,), (0,)), ((), ())), preferred_element_type=F32)


def _split3(x):
    h1 = x.astype(BF16)
    r1 = x - h1.astype(F32)
    h2 = r1.astype(BF16)
    h3 = (r1 - h2.astype(F32)).astype(BF16)
    return h1, h2, h3


def _dot_exact_lhs(w_bf16, x):
    h1, h2, h3 = _split3(x)
    return _dot(w_bf16, h1) + _dot(w_bf16, h2) + _dot(w_bf16, h3)


def _dot_exact_rhs(x, w_bf16):
    h1, h2, h3 = _split3(x)
    return _dot(h1, w_bf16) + _dot(h2, w_bf16) + _dot(h3, w_bf16)


def _scan_chains(r_ref, v_ref, kk_ref, lw_ref, k_ref, a_ref, y_ref, d, reverse):
    L = CHUNK
    n_pairs = r_ref.shape[-1] // LANES
    t_i = lax.broadcasted_iota(jnp.int32, (L, L), 0)
    i_i = lax.broadcasted_iota(jnp.int32, (L, L), 1)
    tri = jnp.where((i_i >= t_i) if reverse else (i_i <= t_i), 1.0, 0.0).astype(BF16)

    lw = lw_ref[...]
    cum = _dot_exact_lhs(tri, lw)
    last = 0 if reverse else L - 1
    tot = cum[last:last + 1, :]
    winv = jnp.exp(-cum)
    wrem = jnp.exp(tot - cum)
    kk = kk_ref[...]
    b = kk * a_ref[...]
    k = k_ref[...]
    rt = r_ref[...] * jnp.exp(cum)
    at = -kk * jnp.exp(cum - lw)
    bt = (b * winv).astype(BF16)
    kt = (k * winv).astype(BF16)
    bh = (b * wrem).astype(BF16)
    kh = (k * wrem).astype(BF16)
    w_tot = jnp.exp(tot)
    v = v_ref[...].astype(BF16)

    t2 = lax.broadcasted_iota(jnp.int32, (L, 2 * LANES), 0)
    i2 = lax.broadcasted_iota(jnp.int32, (L, 2 * LANES), 1) % HEAD
    strict = (i2 > t2) if reverse else (i2 < t2)
    incl = (i2 >= t2) if reverse else (i2 <= t2)
    at, rt = at.astype(BF16), rt.astype(BF16)
    chains = []
    for p in range(n_pairs):
        sl = slice(p * LANES, (p + 1) * LANES)
        chains.append(dict(
            d=d, p=p, sl=sl, y_ref=y_ref, strict=strict, incl=incl, ar=jnp.concatenate([at[:, sl], rt[:, sl]], axis=0),
            bt=bt[:, sl], kt=kt[:, sl], v=v[:, sl], bh=bh[:, sl], kh=kh[:, sl], w_tot=w_tot[:, sl]))
    return chains


def _rwkv_scan_kernel(r_f, v_f, kk_f, lw_f, k_f, a_f, r_b, v_b, kk_b, lw_b, k_b, a_b, yf_ref, yb_ref, s_ref):
    @pl.when(pl.program_id(1) == 0)
    def _():
        s_ref[...] = jnp.zeros_like(s_ref)

    L = CHUNK
    cs = (_scan_chains(r_f, v_f, kk_f, lw_f, k_f, a_f, yf_ref, 0, False)
          + _scan_chains(r_b, v_b, kk_b, lw_b, k_b, a_b, yb_ref, 1, True))
    lo = lax.broadcasted_iota(jnp.int32, (1, LANES), 1) < HEAD
    rr = lax.broadcasted_iota(jnp.int32, (LANES, LANES), 0) < HEAD
    cc = lax.broadcasted_iota(jnp.int32, (LANES, LANES), 1) < HEAD
    same_head = rr == cc
    cat = lambda *xs: jnp.concatenate(xs, axis=0)
    zero = jnp.zeros((), BF16)

    def bd(t):
        t = t.astype(BF16)
        return cat(jnp.where(lo, t, zero), jnp.where(lo, zero, t))

    g = [_dot_nt(c["ar"], cat(bd(c["bt"]), bd(c["kt"]))) for c in cs]
    m = [jnp.where(c["strict"], gi[:L], 0.0).astype(BF16) for c, gi in zip(cs, g)]
    n = [jnp.where(c["incl"], gi[L:], 0.0).astype(BF16) for c, gi in zip(cs, g)]
    s = [s_ref[c["d"], c["p"]] for c in cs]
    ars = [_dot_nt(c["ar"], si.astype(BF16)) for c, si in zip(cs, s)]
    vbd = [bd(c["v"]) for c in cs]
    x = [a[:L] + _dot(mi[:, LANES:], vb) for a, mi, vb in zip(ars, m, vbd)]
    pw = [mi[:, :LANES] for mi in m]
    n_sq = int(math.log2(L))
    for j in range(n_sq):
        if j + 1 < n_sq:
            px = [_dot(pi, jnp.concatenate([bd(xi), bd(pi)], axis=1)) for xi, pi in zip(x, pw)]
            x = [xi + pxi[:, :LANES] for xi, pxi in zip(x, px)]
            pw = [pxi[:, LANES:].astype(BF16) for pxi in px]
        else:
            x = [xi + _dot(pi, bd(xi)) for xi, pi in zip(x, pw)]
    y = [a[L:] + _dot(ni, cat(bd(xi), vb)) for a, ni, xi, vb in zip(ars, n, x, vbd)]
    for c, yi in zip(cs, y):
        c["y_ref"][:, c["sl"]] = yi
    ds = [_dot_tn(cat(xi.astype(BF16), c["v"]), cat(c["bh"], c["kh"])) for c, xi in zip(cs, x)]
    for c, si, di in zip(cs, s, ds):
        s_ref[c["d"], c["p"]] = si * c["w_tot"] + jnp.where(same_head, di, 0.0)


def rwkv_scan(r, v, kk, lw_f, k_f, a_f, lw_b, k_b, a_b, batch):
    m, c = r.shape
    nc = m // batch // CHUNK
    fwd = pl.BlockSpec((CHUNK, c), lambda bi, ci: (bi * nc + ci, 0))
    bwd = pl.BlockSpec((CHUNK, c), lambda bi, ci: (bi * nc + nc - 1 - ci, 0))
    return pl.pallas_call(
        _rwkv_scan_kernel,
        grid=(batch, nc),
        in_specs=[fwd] * 6 + [bwd] * 6,
        out_specs=[fwd, bwd],
        out_shape=[jax.ShapeDtypeStruct((m, c), F32)] * 2,
        scratch_shapes=[pltpu.VMEM((2, c // LANES, LANES, LANES), F32)],
        compiler_params=_cparams("arbitrary", "arbitrary"),
    )(r, v, kk, lw_f, k_f, a_f, r, v, kk, lw_b, k_b, a_b)


def _sigmoid(x):
    return 1.0 / (1.0 + jnp.exp(-x))


def _split2(x):
    hi = x.astype(BF16)
    return hi, (x - hi.astype(F32)).astype(BF16)


def _dot_f32(a, w_hi, w_lo):
    a_hi, a_lo = _split2(a)
    return _dot(a_hi, w_hi) + _dot(a_lo, w_hi) + _dot(a_hi, w_lo)


def _head_block_ones():
    r = lax.broadcasted_iota(jnp.int32, (LANES, LANES), 0) // HEAD
    c = lax.broadcasted_iota(jnp.int32, (LANES, LANES), 1) // HEAD
    return jnp.where(r == c, 1.0, 0.0).astype(BF16)


def _group_sum(x, ones):
    tiles = [_dot_exact_rhs(x[:, j * LANES:(j + 1) * LANES], ones) for j in range(x.shape[1] // LANES)]
    return tiles[0] if len(tiles) == 1 else jnp.concatenate(tiles, axis=1)


def _rmsnorm_kernel(x_ref, g_ref, o_ref):
    x = x_ref[...]
    y = x * lax.rsqrt(jnp.mean(x * x, axis=-1, keepdims=True) + RMS_EPS)
    o_ref[...] = (y * g_ref[...]).astype(o_ref.dtype)


def rmsnorm_bf16(x, g, tm):
    m, d = x.shape
    return pl.pallas_call(
        _rmsnorm_kernel,
        grid=(m // tm,),
        in_specs=[pl.BlockSpec((tm, d), lambda i: (i, 0)), pl.BlockSpec((1, d), lambda i: (0, 0))],
        out_specs=pl.BlockSpec((tm, d), lambda i: (i, 0)),
        out_shape=jax.ShapeDtypeStruct((m, d), BF16),
        compiler_params=_cparams("parallel"),
    )(x, g.reshape(1, d))


def _matmul_kernel(a_ref, w_ref, o_ref, *, act):
    acc = _dot(a_ref[...], w_ref[...])
    if act == "sigmoid":
        acc = _sigmoid(acc)
    o_ref[...] = acc.astype(o_ref.dtype)


def matmul(a, w, out_dtype, tm, tn, act=None):
    m, k = a.shape
    n = w.shape[1]
    return pl.pallas_call(
        functools.partial(_matmul_kernel, act=act),
        grid=(n // tn, m // tm),
        in_specs=[pl.BlockSpec((tm, k), lambda j, i: (i, 0)), pl.BlockSpec((k, tn), lambda j, i: (0, j))],
        out_specs=pl.BlockSpec((tm, tn), lambda j, i: (i, j)),
        out_shape=jax.ShapeDtypeStruct((m, n), out_dtype),
        compiler_params=_cparams("parallel", "parallel"),
    )(a, w)


def _qk_prep_kernel(qk_ref, cos_ref, sin_a_ref, sin_b_ref, gq_ref, gk_ref, q_ref, k_ref):
    ones = _head_block_ones()
    cos, sin_a, sin_b = cos_ref[...], sin_a_ref[...], sin_b_ref[...]
    n_tiles = q_ref.shape[1] // LANES
    for j in range(2 * n_tiles):
        x = qk_ref[:, j * LANES:(j + 1) * LANES]
        ms = _group_sum(x * x, ones) * (1.0 / HEAD)
        is_q = j < n_tiles
        y = x * lax.rsqrt(ms + RMS_EPS) * (gq_ref[...] if is_q else gk_ref[...])
        y = y * cos + pltpu.roll(y, LANES - ROPE_DIM // 2, 1) * sin_a + pltpu.roll(y, ROPE_DIM // 2, 1) * sin_b
        if is_q:
            q_ref[:, j * LANES:(j + 1) * LANES] = (y * HEAD ** -0.5).astype(q_ref.dtype)
        else:
            jj = j - n_tiles
            k_ref[:, jj * LANES:(jj + 1) * LANES] = y.astype(k_ref.dtype)


def qk_prep(qk, cos, sin_a, sin_b, gq, gk, seq, tm):
    m, w2 = qk.shape
    w = w2 // 2
    st = seq // tm
    tab = pl.BlockSpec((tm, LANES), lambda i: (i % st, 0))
    vec = pl.BlockSpec((1, LANES), lambda i: (0, 0))
    return pl.pallas_call(
        _qk_prep_kernel,
        grid=(m // tm,),
        in_specs=[pl.BlockSpec((tm, w2), lambda i: (i, 0)), tab, tab, tab, vec, vec],
        out_specs=[pl.BlockSpec((tm, w), lambda i: (i, 0))] * 2,
        out_shape=[jax.ShapeDtypeStruct((m, w), BF16)] * 2,
        compiler_params=_cparams("parallel"),
    )(qk, cos, sin_a, sin_b, gq, gk)


KEY_CHUNK = 512
MAX_SCORE_BOUND = 40.0


def _attn_kernel(bound_ref, lam_ref, q_ref, k_ref, v_ref, g_ref, o_ref, vaug_ref, *, lam_init):
    lp = lam_ref[...]
    lam = (jnp.exp(jnp.sum(lp[0:1] * lp[1:2], axis=-1, keepdims=True))
           - jnp.exp(jnp.sum(lp[2:3] * lp[3:4], axis=-1, keepdims=True)) + lam_init)
    seq = k_ref.shape[0]
    tq = q_ref.shape[0]
    tk = min(KEY_CHUNK, seq)

    @pl.when(pl.program_id(2) == 0)
    def _():
        vaug_ref[:, :LANES] = v_ref[...]
        vaug_ref[:, LANES:] = jnp.ones((seq, LANES), BF16)

    q = q_ref[...]
    lo = lax.broadcasted_iota(jnp.int32, (1, LANES), 1) < HEAD
    zero = jnp.zeros_like(q)
    q_lo, q_hi = jnp.where(lo, q, zero), jnp.where(lo, zero, q)

    def finish(o2):
        o = o2[:tq] - lam * o2[tq:]
        o = o * lax.rsqrt(jnp.mean(o * o, axis=-1, keepdims=True) + RMS_EPS)
        o_ref[...] = (o * g_ref[...] * (1.0 - lam_init)).astype(o_ref.dtype)

    bound = bound_ref[0, 0]
    safe = bound <= MAX_SCORE_BOUND

    @pl.when(safe)
    def _():
        q2 = jnp.concatenate([q_lo, q_hi], axis=0)
        acc = jnp.zeros((2 * tq, 2 * LANES), F32)
        for c in range(seq // tk):
            s = _dot_nt(q2, k_ref[c * tk:(c + 1) * tk, :])
            acc = acc + _dot(jnp.exp(s - bound).astype(BF16), vaug_ref[c * tk:(c + 1) * tk, :])
        finish(acc[:, :LANES] / acc[:, LANES:LANES + 1])

    @pl.when(jnp.logical_not(safe))
    def _():
        def branch(qm):
            s = _dot_nt(qm, k_ref[...])
            p = jnp.exp(s - jnp.max(s, axis=-1, keepdims=True))
            return _dot(p.astype(BF16), v_ref[...]) / jnp.sum(p, axis=-1, keepdims=True)

        finish(jnp.concatenate([branch(q_lo), branch(q_hi)], axis=0))


def diff_attention(q, k, v, score_bound, lam_params, subln_g, batch, lam_init, tq):
    m, w = q.shape
    seq = m // batch
    nq = seq // tq
    return pl.pallas_call(
        functools.partial(_attn_kernel, lam_init=lam_init),
        grid=(batch, w // LANES, nq),
        in_specs=[pl.BlockSpec(memory_space=pltpu.SMEM),
                  pl.BlockSpec((4, HEAD), lambda b, h, i: (0, 0)),
                  pl.BlockSpec((tq, LANES), lambda b, h, i: (b * nq + i, h)),
                  pl.BlockSpec((seq, LANES), lambda b, h, i: (b, h)),
                  pl.BlockSpec((seq, LANES), lambda b, h, i: (b, h)),
                  pl.BlockSpec((1, LANES), lambda b, h, i: (0, 0))],
        out_specs=pl.BlockSpec((tq, LANES), lambda b, h, i: (b * nq + i, h)),
        out_shape=jax.ShapeDtypeStruct((m, w), BF16),
        scratch_shapes=[pltpu.VMEM((seq, 2 * LANES), BF16)],
        compiler_params=_cparams("parallel", "parallel", "arbitrary"),
    )(score_bound, lam_params, q, k, v, subln_g.reshape(1, LANES))


def _rwkv_prep_kernel(p_ref, prev_ref, next_ref, mu_ref, w0_ref, w2h_ref, w2l_ref, a0_ref, a2h_ref, a2l_ref,
                      g2h_ref, g2l_ref, kkw_ref, ka_ref, rk_ref,
                      r_o, v_o, kk_o, kf_o, kb_o, lwf_o, lwb_o, af_o, ab_o, bonus_o, g_o, *, seq_tiles):
    tm = p_ref.shape[0]
    c = r_o.shape[1]
    ti = pl.program_id(0) % seq_tiles
    p = p_ref[...]
    row = lax.broadcasted_iota(jnp.int32, (tm, 1), 0)
    prev_row = jnp.where(ti > 0, prev_ref[7:8, :], 0.0)
    next_row = jnp.where(ti < seq_tiles - 1, next_ref[0:1, :], 0.0)
    prev = jnp.where(row == 0, prev_row, pltpu.roll(p, 1, 0))
    nxt = jnp.where(row == tm - 1, next_row, pltpu.roll(p, tm - 1, 0))
    p = p + mu_ref[...] * (0.5 * (prev + nxt) - p)

    r, k, v = p[:, :c], p[:, c:2 * c], p[:, 2 * c:3 * c]
    wd = p[:, 3 * c:3 * c + LANES]
    ad = p[:, 3 * c + LANES:3 * c + 2 * LANES]
    gd = p[:, 3 * c + 2 * LANES:]
    wl = w0_ref[...] + _dot_f32(jnp.tanh(wd), w2h_ref[...], w2l_ref[...])
    lw = -math.exp(-0.5) * _sigmoid(wl)
    a = _sigmoid(a0_ref[...] + _dot_f32(ad, a2h_ref[...], a2l_ref[...]))
    g_o[...] = _dot_f32(_sigmoid(gd), g2h_ref[...], g2l_ref[...])
    a_f, a_b = a[:, :c], a[:, c:]
    ones = _head_block_ones()
    kk = k * kkw_ref[...]
    kk = kk / jnp.maximum(jnp.sqrt(_group_sum(kk * kk, ones)), 1e-12)
    k_a = ka_ref[...]
    k_f = k * (1.0 + (a_f - 1.0) * k_a)
    k_b = k * (1.0 + (a_b - 1.0) * k_a)
    bonus_o[...] = _group_sum(r * (k_f + k_b) * rk_ref[...], ones) * v
    r_o[...] = r
    v_o[...] = v
    kk_o[...] = kk
    kf_o[...] = k_f
    kb_o[...] = k_b
    lwf_o[...] = lw[:, :c]
    lwb_o[...] = lw[:, c:]
    af_o[...] = a_f
    ab_o[...] = a_b


def rwkv_prep(p, mu, w0, w2, a0, a2, g2, k_k, k_a, r_k, seq, tm, c):
    m, pc = p.shape
    st = seq // tm
    hb = tm // 8
    last = m // 8 - 1
    full = lambda a: pl.BlockSpec(a.shape, lambda i: (0, 0))
    row_c = pl.BlockSpec((tm, c), lambda i: (i, 0))
    small = [mu, w0, w2[0], w2[1], a0, a2[0], a2[1], g2[0], g2[1], k_k, k_a, r_k]
    return pl.pallas_call(
        functools.partial(_rwkv_prep_kernel, seq_tiles=st),
        grid=(m // tm,),
        in_specs=[pl.BlockSpec((tm, pc), lambda i: (i, 0)),
                  pl.BlockSpec((8, pc), lambda i: (jnp.maximum(i * hb - 1, 0), 0)),
                  pl.BlockSpec((8, pc), lambda i: (jnp.minimum((i + 1) * hb, last), 0))] + [full(a) for a in small],
        out_specs=[row_c] * 11,
        out_shape=[jax.ShapeDtypeStruct((m, c), F32)] * 11,
        compiler_params=_cparams("parallel"),
    )(p, p, p, *small)


def _rwkv_post_kernel(yf_ref, yb_ref, bonus_ref, g_ref, lnw_ref, lnb_ref, o_ref):
    ones = _head_block_ones()
    y = yf_ref[...] + yb_ref[...]
    mean = _group_sum(y, ones) * (1.0 / HEAD)
    yc = y - mean
    var = _group_sum(yc * yc, ones) * (1.0 / HEAD)
    yn = yc * lax.rsqrt(var + GN_EPS) * lnw_ref[...] + lnb_ref[...]
    o_ref[...] = ((yn + bonus_ref[...]) * g_ref[...]).astype(o_ref.dtype)


def rwkv_post(yf, yb, bonus, g, ln_w, ln_b, tm):
    m, c = yf.shape
    row = pl.BlockSpec((tm, c), lambda i: (i, 0))
    vec = pl.BlockSpec((1, c), lambda i: (0, 0))
    return pl.pallas_call(
        _rwkv_post_kernel,
        grid=(m // tm,),
        in_specs=[row] * 4 + [vec] * 2,
        out_specs=row,
        out_shape=jax.ShapeDtypeStruct((m, c), BF16),
        compiler_params=_cparams("parallel"),
    )(yf, yb, bonus, g, ln_w.reshape(1, c), ln_b.reshape(1, c))


def _merge_kernel(ya_ref, yb_ref, gate_ref, x_ref, wa_ref, wb_ref, wo_ref, fg_ref, wrh_ref, wrl_ref,
                  h_ref, hn_ref, aff_ref):
    d = x_ref.shape[1]
    merged = (gate_ref[:, :d].astype(F32) * _dot(ya_ref[...], wa_ref[...])
              + gate_ref[:, d:].astype(F32) * _dot(yb_ref[...], wb_ref[...]))
    h = x_ref[...] + _dot(merged.astype(BF16), wo_ref[...])
    h_ref[...] = h
    hn = h * lax.rsqrt(jnp.mean(h * h, axis=-1, keepdims=True) + RMS_EPS) * fg_ref[...]
    hn_ref[...] = hn.astype(hn_ref.dtype)
    hn_hi, hn_lo = _split2(hn)
    wr_hi = wrh_ref[...]
    logits = _dot_nt(wr_hi, hn_hi) + _dot_nt(wr_hi, hn_lo) + _dot_nt(wrl_ref[...], hn_hi)
    e = jnp.exp(logits - jnp.max(logits, axis=0, keepdims=True))
    aff_ref[...] = e / jnp.sum(e, axis=0, keepdims=True)


def merge_out_router(ya, yb, gates, x, wa, wb, wo, ffn_g, wr_hi, wr_lo, tm):
    m, d = x.shape
    full = lambda a: pl.BlockSpec(a.shape, lambda i: (0, 0))
    row = lambda a: pl.BlockSpec((tm, a.shape[1]), lambda i: (i, 0))
    ne = wr_hi.shape[0]
    return pl.pallas_call(
        _merge_kernel,
        grid=(m // tm,),
        in_specs=[row(ya), row(yb), row(gates), row(x), full(wa), full(wb), full(wo), full(ffn_g),
                  full(wr_hi), full(wr_lo)],
        out_specs=[pl.BlockSpec((tm, d), lambda i: (i, 0)), pl.BlockSpec((tm, d), lambda i: (i, 0)),
                   pl.BlockSpec((ne, tm), lambda i: (0, i))],
        out_shape=[jax.ShapeDtypeStruct((m, d), F32), jax.ShapeDtypeStruct((m, d), BF16),
                   jax.ShapeDtypeStruct((ne, m), F32)],
        compiler_params=_cparams("parallel"),
    )(ya, yb, gates, x, wa, wb, wo, ffn_g, wr_hi, wr_lo)


def _prefix_excl(m_bf16, upper):
    rows, t = m_bf16.shape
    off = jnp.zeros((rows, 1), F32)
    out = []
    for j in range(t // LANES):
        blk = m_bf16[:, j * LANES:(j + 1) * LANES]
        out.append(_dot(blk, upper) + off)
        off = off + jnp.sum(blk.astype(F32), axis=-1, keepdims=True)
    return jnp.concatenate(out, axis=1)


TOKEN_BLOCK = 256
SLOT_WINDOW = 128
SLOT_ALIGN = 16


def _select_kernel(aff_ref, pos_ref, cnt_ref, *, cap):
    aff = aff_ref[...]
    count = lambda pred: jnp.sum(jnp.where(pred, 1.0, 0.0), axis=-1, keepdims=True)
    as_f32 = lambda bits: pltpu.bitcast(jnp.broadcast_to(bits, aff.shape), F32)

    def step(i, thr):
        cand = thr | (jnp.int32(1) << (30 - i))
        return jnp.where(count(aff >= as_f32(cand)) >= cap, cand, thr)

    thr = lax.fori_loop(0, 31, step, jnp.zeros((aff.shape[0], 1), jnp.int32))
    gt = aff >= as_f32(thr + 1)
    eq = (aff >= as_f32(thr)) & jnp.logical_not(gt)
    r = lax.broadcasted_iota(jnp.int32, (LANES, LANES), 0)
    c = lax.broadcasted_iota(jnp.int32, (LANES, LANES), 1)
    upper = jnp.where(r < c, 1.0, 0.0).astype(BF16)
    need = cap - jnp.sum(jnp.where(gt, 1.0, 0.0), axis=-1, keepdims=True)
    eq_rank = _prefix_excl(jnp.where(eq, 1.0, 0.0).astype(BF16), upper)
    sel = gt | (eq & (eq_rank < need))
    sel_bf = jnp.where(sel, 1.0, 0.0).astype(BF16)
    pos = _prefix_excl(sel_bf, upper)
    pos_ref[...] = jnp.where(sel, pos, -1.0).astype(jnp.int32)
    seq = aff.shape[1]
    tok = lax.broadcasted_iota(jnp.int32, (seq, LANES), 0)
    blk = lax.broadcasted_iota(jnp.int32, (seq, LANES), 1)
    before = jnp.where(tok < blk * TOKEN_BLOCK, 1.0, 0.0).astype(BF16)
    cnt_ref[0] = _dot(sel_bf, before).astype(jnp.int32)


def select_slots(aff_t, batch, cap):
    ne, m = aff_t.shape
    seq = m // batch
    assert seq // TOKEN_BLOCK < LANES
    return pl.pallas_call(
        functools.partial(_select_kernel, cap=cap),
        grid=(batch,),
        in_specs=[pl.BlockSpec((ne, seq), lambda b: (0, b))],
        out_specs=[pl.BlockSpec((ne, seq), lambda b: (0, b)), pl.BlockSpec((1, ne, LANES), lambda b: (b, 0, 0))],
        out_shape=[jax.ShapeDtypeStruct((ne, m), jnp.int32), jax.ShapeDtypeStruct((batch, ne, LANES), jnp.int32)],
        compiler_params=_cparams("parallel"),
    )(aff_t)


def _for_each_window(cnt_ref, base, n_blocks, sw, cap, step):
    def start(j):
        return (cnt_ref[base + j] // SLOT_ALIGN) * SLOT_ALIGN

    def hit_fn(lo):
        s0 = pl.multiple_of(jnp.minimum(lo, cap - sw), SLOT_ALIGN)
        slot = s0 + lax.broadcasted_iota(jnp.int32, (sw, 1), 0)
        return s0, lambda pos_blk: (pos_blk == slot) & (slot >= lo)

    for j in range(n_blocks):
        step(j, *hit_fn(start(j)))
    for j in range(n_blocks):
        lo0 = start(j)
        n_win = (cnt_ref[base + j + 1] - lo0 + sw - 1) // sw

        def extra(k, carry, j=j, lo0=lo0):
            step(j, *hit_fn(lo0 + k * sw))
            return carry

        lax.fori_loop(1, n_win, extra, 0)


def _gather_kernel(cnt_ref, pos_ref, aff_ref, hn_ref, xe_ref, gate_ref, acc_ref, gacc_ref):
    seq = hn_ref.shape[0]
    cap = acc_ref.shape[0]
    tb = min(TOKEN_BLOCK, seq)
    sw = min(SLOT_WINDOW, cap)
    base = (pl.program_id(0) * pl.num_programs(1) + pl.program_id(1)) * LANES
    acc_ref[...] = jnp.zeros_like(acc_ref)
    gacc_ref[...] = jnp.zeros_like(gacc_ref)

    def step(j, s0, hit_of):
        tok = slice(j * tb, (j + 1) * tb)
        hit = hit_of(pos_ref[0, :, tok])
        rows = pl.ds(s0, sw)
        acc_ref[rows, :] += _dot(jnp.where(hit, 1.0, 0.0).astype(BF16), hn_ref[tok, :])
        gacc_ref[rows, :] += jnp.sum(jnp.where(hit, aff_ref[0, :, tok], 0.0), axis=-1, keepdims=True)

    _for_each_window(cnt_ref, base, seq // tb, sw, cap, step)
    xe_ref[0, 0] = acc_ref[...].astype(xe_ref.dtype)
    gate_ref[0, 0] = gacc_ref[...]


def moe_gather(cnt, pos3, aff3, hn, batch, cap):
    ne = pos3.shape[0]
    m, d = hn.shape
    seq = m // batch
    assert cap % min(SLOT_WINDOW, cap) == 0
    row = pl.BlockSpec((1, 1, seq), lambda b, e, c: (e, 0, b))
    return pl.pallas_call(
        _gather_kernel,
        grid_spec=pltpu.PrefetchScalarGridSpec(
            num_scalar_prefetch=1,
            grid=(batch, ne),
            in_specs=[row, row, pl.BlockSpec((seq, d), lambda b, e, c: (b, 0))],
            out_specs=[pl.BlockSpec((1, 1, cap, d), lambda b, e, c: (e, b, 0, 0)),
                       pl.BlockSpec((1, 1, cap, 1), lambda b, e, c: (e, b, 0, 0))],
            scratch_shapes=[pltpu.VMEM((cap, d), F32), pltpu.VMEM((cap, 1), F32)]),
        out_shape=[jax.ShapeDtypeStruct((ne, batch, cap, d), BF16),
                   jax.ShapeDtypeStruct((ne, batch, cap, 1), F32)],
        compiler_params=_cparams("parallel", "parallel"),
    )(cnt, pos3, aff3, hn)


def _expert_kernel(xe_ref, gate_ref, wg_ref, wu_ref, hid_ref, wg_bf, wu_bf):
    @pl.when(pl.program_id(1) == 0)
    def _():
        wg_bf[...] = wg_ref[0].astype(BF16)
        wu_bf[...] = wu_ref[0].astype(BF16)

    xe = xe_ref[0, 0]
    hg = _dot(xe, wg_bf[...])
    hu = _dot(xe, wu_bf[...])
    hid_ref[0, 0] = (hg * _sigmoid(hg) * hu * gate_ref[0, 0]).astype(hid_ref.dtype)


def moe_experts(xe, gate, wg, wu):
    ne, batch, cap, d = xe.shape
    ff = wg.shape[2]
    return pl.pallas_call(
        _expert_kernel,
        grid=(ne, batch),
        in_specs=[pl.BlockSpec((1, 1, cap, d), lambda e, b: (e, b, 0, 0)),
                  pl.BlockSpec((1, 1, cap, 1), lambda e, b: (e, b, 0, 0)),
                  pl.BlockSpec((1, d, ff), lambda e, b: (e, 0, 0)),
                  pl.BlockSpec((1, d, ff), lambda e, b: (e, 0, 0))],
        out_specs=pl.BlockSpec((1, 1, cap, ff), lambda e, b: (e, b, 0, 0)),
        out_shape=jax.ShapeDtypeStruct((ne, batch, cap, ff), BF16),
        scratch_shapes=[pltpu.VMEM((d, ff), BF16)] * 2,
        compiler_params=_cparams("parallel", "arbitrary"),
    )(xe, gate, wg, wu)


def _down_kernel(hid_ref, wd_ref, ye_ref, wd_bf):
    @pl.when(pl.program_id(1) == 0)
    def _():
        wd_bf[...] = wd_ref[0].astype(BF16)

    ye_ref[0, 0] = _dot(hid_ref[0, 0], wd_bf[...]).astype(ye_ref.dtype)


def moe_down(hid, wd):
    ne, batch, cap, ff = hid.shape
    d = wd.shape[2]
    return pl.pallas_call(
        _down_kernel,
        grid=(ne, batch),
        in_specs=[pl.BlockSpec((1, 1, cap, ff), lambda e, b: (e, b, 0, 0)),
                  pl.BlockSpec((1, ff, d), lambda e, b: (e, 0, 0))],
        out_specs=pl.BlockSpec((1, 1, cap, d), lambda e, b: (e, b, 0, 0)),
        out_shape=jax.ShapeDtypeStruct((ne, batch, cap, d), BF16),
        scratch_shapes=[pltpu.VMEM((ff, d), BF16)],
        compiler_params=_cparams("parallel", "arbitrary"),
    )(hid, wd)


def _scatter_kernel(cnt_ref, pos_ref, ye_ref, h_ref, o_ref):
    e = pl.program_id(2)
    tt = h_ref.shape[0]
    cap = ye_ref.shape[2]
    tb = min(TOKEN_BLOCK, tt)
    sw = min(SLOT_WINDOW, cap)
    base = (pl.program_id(0) * pl.num_programs(2) + e) * LANES + pl.program_id(1) * (tt // tb)

    @pl.when(e == 0)
    def _():
        o_ref[...] = h_ref[...]

    def step(j, s0, hit_of):
        tok = slice(j * tb, (j + 1) * tb)
        hit = hit_of(pos_ref[0, :, tok])
        o_ref[tok, :] += _dot_tn(jnp.where(hit, 1.0, 0.0).astype(BF16), ye_ref[0, 0, pl.ds(s0, sw), :])

    _for_each_window(cnt_ref, base, tt // tb, sw, cap, step)


def moe_scatter(cnt, pos3, ye, h, batch, tt):
    ne, _, cap, d = ye.shape
    m = h.shape[0]
    nt = m // batch // tt
    return pl.pallas_call(
        _scatter_kernel,
        grid_spec=pltpu.PrefetchScalarGridSpec(
            num_scalar_prefetch=1,
            grid=(batch, nt, ne),
            in_specs=[pl.BlockSpec((1, 1, tt), lambda b, i, e, c: (e, 0, b * nt + i)),
                      pl.BlockSpec((1, 1, cap, d), lambda b, i, e, c: (e, b, 0, 0)),
                      pl.BlockSpec((tt, d), lambda b, i, e, c: (b * nt + i, 0))],
            out_specs=pl.BlockSpec((tt, d), lambda b, i, e, c: (b * nt + i, 0))),
        out_shape=jax.ShapeDtypeStruct((m, d), F32),
        compiler_params=_cparams("parallel", "parallel", "arbitrary"),
    )(cnt, pos3, ye, h)


def _rope_tables(seq):
    half = ROPE_DIM // 2
    inv = ROPE_THETA ** (-(jnp.arange(0, ROPE_DIM, 2, dtype=F32) / ROPE_DIM))
    ang = jnp.arange(seq, dtype=F32)[:, None] * inv[None, :]
    cos, sin = jnp.cos(ang), jnp.sin(ang)
    one = jnp.ones((seq, HEAD - ROPE_DIM), F32)
    zero = lambda n: jnp.zeros((seq, n), F32)
    cos_t = jnp.concatenate([cos, cos, one], axis=1)
    sin_a = jnp.concatenate([-sin, zero(HEAD - half)], axis=1)
    sin_b = jnp.concatenate([zero(half), sin, zero(HEAD - ROPE_DIM)], axis=1)
    return [jnp.tile(t, (1, LANES // HEAD)) for t in (cos_t, sin_a, sin_b)]


def _hi_lo(w):
    hi = w.astype(BF16)
    return hi, (w - hi.astype(F32)).astype(BF16)


def _block_diag2(wf, wb):
    z = jnp.zeros_like(wf)
    return jnp.concatenate([jnp.concatenate([wf, z], axis=1), jnp.concatenate([z, wb], axis=1)], axis=0)


def _col_tile(n):
    for t in (1024, 896, 512, 256, 128):
        if n % t == 0:
            return t
    raise ValueError(f"unsupported matmul width {n}")


def kernel(x, attn_norm_g, w_in, q_norm_g, k_norm_g, lambda_q1, lambda_k1, lambda_q2, lambda_k2, subln_g, shift_mu, w0_f, w2_f, w0_b, w2_b, a0_f, a2_f, a0_b, a2_b, g2, k_k, k_a, r_k, ln_x_w, ln_x_b, w_branch_a, w_branch_b, w_out, ffn_norm_g, w_router, w_gate_e, w_up_e, w_down_e):
    batch, seq, d = x.shape
    m = batch * seq
    depth = w_in.shape[0]
    c = w_branch_b.shape[1]
    qk_w = 2 * w_branch_a.shape[1]
    v_w = w_branch_a.shape[1]
    rw_cols = shift_mu.shape[1]
    rw_pad = -(-rw_cols // LANES) * LANES
    ne = w_router.shape[2]
    cap = 2 * seq // ne
    tm = min(256, seq)
    tmm = 512 if m % 512 == 0 else tm
    cos_t, sin_a, sin_b = _rope_tables(seq)
    row = lambda a: a.reshape(1, -1)

    h = x.reshape(m, d)
    for l in range(depth):
        lam_init = 0.8 - 0.6 * math.exp(-0.3 * l)
        w = w_in[l]
        w_qk = w[:, :qk_w].astype(BF16)
        w_v = w[:, qk_w:qk_w + v_w].astype(BF16)
        w_rw = jnp.pad(w[:, qk_w + v_w:qk_w + v_w + rw_cols], ((0, 0), (0, rw_pad - rw_cols))).astype(BF16)
        w_gt = w[:, qk_w + v_w + rw_cols:].astype(BF16)

        hn = rmsnorm_bf16(h, attn_norm_g[l], tm)
        qk = matmul(hn, w_qk, F32, tmm, _col_tile(qk_w))
        v = matmul(hn, w_v, BF16, tmm, _col_tile(v_w))
        p_rw = matmul(hn, w_rw, F32, tmm, _col_tile(rw_pad))
        gates = matmul(hn, w_gt, BF16, tmm, _col_tile(2 * d), act="sigmoid")

        gq = row(jnp.tile(q_norm_g[l], LANES // HEAD))
        gk = row(jnp.tile(k_norm_g[l], LANES // HEAD))
        q, k = qk_prep(qk, cos_t, sin_a, sin_b, gq, gk, seq, tm)
        lam_params = jnp.stack([lambda_q1[l], lambda_k1[l], lambda_q2[l], lambda_k2[l]])
        score_bound = (1.02 * math.sqrt(HEAD) * jnp.max(jnp.abs(q_norm_g[l])) * jnp.max(jnp.abs(k_norm_g[l]))
                       ).astype(F32).reshape(1, 1)
        y_a = diff_attention(q, k, v, score_bound, lam_params, subln_g[l], batch, lam_init, min(512, seq))

        mu = jnp.pad(shift_mu[l], (0, rw_pad - rw_cols)).reshape(1, rw_pad)
        w0 = row(jnp.concatenate([w0_f[l], w0_b[l]]))
        a0 = row(jnp.concatenate([a0_f[l], a0_b[l]]))
        w2 = _hi_lo(_block_diag2(w2_f[l], w2_b[l]))
        a2 = _hi_lo(_block_diag2(a2_f[l], a2_b[l]))
        g_rows = rw_pad - 3 * c - 2 * LANES
        g2p = _hi_lo(jnp.pad(g2[l], ((0, g_rows - g2.shape[1]), (0, 0))))
        r, vv, kk, k_f, k_b, lw_f, lw_b, a_f, a_b, bonus, g = rwkv_prep(
            p_rw, mu, w0, w2, a0, a2, g2p, row(k_k[l]), row(k_a[l]), row(r_k[l]), seq, tm, c)
        y_f, y_bk = rwkv_scan(r, vv, kk, lw_f, k_f, a_f, lw_b, k_b, a_b, batch)
        y_b = rwkv_post(y_f, y_bk, bonus, g, ln_x_w[l], ln_x_b[l], tm)

        wr_hi, wr_lo = _hi_lo(w_router[l].T)
        h2, hn2, aff_t = merge_out_router(
            y_a, y_b, gates, h, w_branch_a[l].astype(BF16), w_branch_b[l].astype(BF16), w_out[l].astype(BF16),
            row(ffn_norm_g[l]), wr_hi, wr_lo, tm)

        pos, cnt = select_slots(aff_t, batch, cap)
        pos3 = pos.reshape(ne, 1, m)
        cnt = cnt.reshape(-1)
        xe, gate = moe_gather(cnt, pos3, aff_t.reshape(ne, 1, m), hn2, batch, cap)
        hid = moe_experts(xe, gate, w_gate_e[l], w_up_e[l])
        ye = moe_down(hid, w_down_e[l])
        h = moe_scatter(cnt, pos3, ye, h2, batch, min(1024, seq))
    return h.reshape(batch, seq, d)
```

```python
import functools
import math

import jax
import jax.numpy as jnp
from jax import lax
from jax.experimental import pallas as pl
from jax.experimental.pallas import tpu as pltpu

F32 = jnp.float32
BF16 = jnp.bfloat16

LANES = 128
HEAD = 64
CHUNK = 64
N_EXPERTS = 16
RMS_EPS = 1e-6
GN_EPS = 64e-5
ROPE_THETA = 500000.0
ROPE_DIM = 16
VMEM_LIMIT = 56 * 1024 * 1024


def _cparams(*sem):
    return pltpu.CompilerParams(dimension_semantics=sem, vmem_limit_bytes=VMEM_LIMIT)


def _dot(a, b):
    return jnp.dot(a, b, preferred_element_type=F32)


def _dot_nt(a, b):
    return lax.dot_general(a, b, (((1,), (1,)), ((), ())), preferred_element_type=F32)


def _dot_tn(a, b):
    return lax.dot_general(a, b, (((0,), (0,)), ((), ())), preferred_element_type=F32)


def _split3(x):
    h1 = x.astype(BF16)
    r1 = x - h1.astype(F32)
    h2 = r1.astype(BF16)
    h3 = (r1 - h2.astype(F32)).astype(BF16)
    return h1, h2, h3


def _dot_exact_lhs(w_bf16, x):
    h1, h2, h3 = _split3(x)
    return _dot(w_bf16, h1) + _dot(w_bf16, h2) + _dot(w_bf16, h3)


def _dot_exact_rhs(x, w_bf16):
    h1, h2, h3 = _split3(x)
    return _dot(h1, w_bf16) + _dot(h2, w_bf16) + _dot(h3, w_bf16)


def _scan_chains(r_ref, v_ref, kk_ref, lw_ref, k_ref, a_ref, y_ref, d, reverse):
    L = CHUNK
    n_pairs = r_ref.shape[-1] // LANES
    t_i = lax.broadcasted_iota(jnp.int32, (L, L), 0)
    i_i = lax.broadcasted_iota(jnp.int32, (L, L), 1)
    tri = jnp.where((i_i >= t_i) if reverse else (i_i <= t_i), 1.0, 0.0).astype(BF16)

    lw = lw_ref[...]
    cum = _dot_exact_lhs(tri, lw)
    last = 0 if reverse else L - 1
    tot = cum[last:last + 1, :]
    winv = jnp.exp(-cum)
    wrem = jnp.exp(tot - cum)
    kk = kk_ref[...]
    b = kk * a_ref[...]
    k = k_ref[...]
    rt = r_ref[...] * jnp.exp(cum)
    at = -kk * jnp.exp(cum - lw)
    bt = (b * winv).astype(BF16)
    kt = (k * winv).astype(BF16)
    bh = (b * wrem).astype(BF16)
    kh = (k * wrem).astype(BF16)
    w_tot = jnp.exp(tot)
    v = v_ref[...].astype(BF16)

    t2 = lax.broadcasted_iota(jnp.int32, (L, 2 * LANES), 0)
    i2 = lax.broadcasted_iota(jnp.int32, (L, 2 * LANES), 1) % HEAD
    strict = (i2 > t2) if reverse else (i2 < t2)
    incl = (i2 >= t2) if reverse else (i2 <= t2)
    at, rt = at.astype(BF16), rt.astype(BF16)
    chains = []
    for p in range(n_pairs):
        sl = slice(p * LANES, (p + 1) * LANES)
        chains.append(dict(
            d=d, p=p, sl=sl, y_ref=y_ref, strict=strict, incl=incl, ar=jnp.concatenate([at[:, sl], rt[:, sl]], axis=0),
            bt=bt[:, sl], kt=kt[:, sl], v=v[:, sl], bh=bh[:, sl], kh=kh[:, sl], w_tot=w_tot[:, sl]))
    return chains


def _rwkv_scan_kernel(r_f, v_f, kk_f, lw_f, k_f, a_f, r_b, v_b, kk_b, lw_b, k_b, a_b, yf_ref, yb_ref, s_ref):
    @pl.when(pl.program_id(1) == 0)
    def _():
        s_ref[...] = jnp.zeros_like(s_ref)

    L = CHUNK
    cs = (_scan_chains(r_f, v_f, kk_f, lw_f, k_f, a_f, yf_ref, 0, False)
          + _scan_chains(r_b, v_b, kk_b, lw_b, k_b, a_b, yb_ref, 1, True))
    lo = lax.broadcasted_iota(jnp.int32, (1, LANES), 1) < HEAD
    rr = lax.broadcasted_iota(jnp.int32, (LANES, LANES), 0) < HEAD
    cc = lax.broadcasted_iota(jnp.int32, (LANES, LANES), 1) < HEAD
    same_head = rr == cc
    cat = lambda *xs: jnp.concatenate(xs, axis=0)
    zero = jnp.zeros((), BF16)

    def bd(t):
        t = t.astype(BF16)
        return cat(jnp.where(lo, t, zero), jnp.where(lo, zero, t))

    g = [_dot_nt(c["ar"], cat(bd(c["bt"]), bd(c["kt"]))) for c in cs]
    m = [jnp.where(c["strict"], gi[:L], 0.0).astype(BF16) for c, gi in zip(cs, g)]
    n = [jnp.where(c["incl"], gi[L:], 0.0).astype(BF16) for c, gi in zip(cs, g)]
    s = [s_ref[c["d"], c["p"]] for c in cs]
    ars = [_dot_nt(c["ar"], si.astype(BF16)) for c, si in zip(cs, s)]
    vbd = [bd(c["v"]) for c in cs]
    x = [a[:L] + _dot(mi[:, LANES:], vb) for a, mi, vb in zip(ars, m, vbd)]
    pw = [mi[:, :LANES] for mi in m]
    n_sq = int(math.log2(L))
    for j in range(n_sq):
        if j + 1 < n_sq:
            px = [_dot(pi, jnp.concatenate([bd(xi), bd(pi)], axis=1)) for xi, pi in zip(x, pw)]
            x = [xi + pxi[:, :LANES] for xi, pxi in zip(x, px)]
            pw = [pxi[:, LANES:].astype(BF16) for pxi in px]
        else:
            x = [xi + _dot(pi, bd(xi)) for xi, pi in zip(x, pw)]
    y = [a[L:] + _dot(ni, cat(bd(xi), vb)) for a, ni, xi, vb in zip(ars, n, x, vbd)]
    for c, yi in zip(cs, y):
        c["y_ref"][:, c["sl"]] = yi
    ds = [_dot_tn(cat(xi.astype(BF16), c["v"]), cat(c["bh"], c["kh"])) for c, xi in zip(cs, x)]
    for c, si, di in zip(cs, s, ds):
        s_ref[c["d"], c["p"]] = si * c["w_tot"] + jnp.where(same_head, di, 0.0)


def rwkv_scan(r, v, kk, lw_f, k_f, a_f, lw_b, k_b, a_b, batch):
    m, c = r.shape
    nc = m // batch // CHUNK
    fwd = pl.BlockSpec((CHUNK, c), lambda bi, ci: (bi * nc + ci, 0))
    bwd = pl.BlockSpec((CHUNK, c), lambda bi, ci: (bi * nc + nc - 1 - ci, 0))
    return pl.pallas_call(
        _rwkv_scan_kernel,
        grid=(batch, nc),
        in_specs=[fwd] * 6 + [bwd] * 6,
        out_specs=[fwd, bwd],
        out_shape=[jax.ShapeDtypeStruct((m, c), F32)] * 2,
        scratch_shapes=[pltpu.VMEM((2, c // LANES, LANES, LANES), F32)],
        compiler_params=_cparams("arbitrary", "arbitrary"),
    )(r, v, kk, lw_f, k_f, a_f, r, v, kk, lw_b, k_b, a_b)


def _sigmoid(x):
    return 1.0 / (1.0 + jnp.exp(-x))


def _split2(x):
    hi = x.astype(BF16)
    return hi, (x - hi.astype(F32)).astype(BF16)


def _dot_f32(a, w_hi, w_lo):
    a_hi, a_lo = _split2(a)
    return _dot(a_hi, w_hi) + _dot(a_lo, w_hi) + _dot(a_hi, w_lo)


def _head_block_ones():
    r = lax.broadcasted_iota(jnp.int32, (LANES, LANES), 0) // HEAD
    c = lax.broadcasted_iota(jnp.int32, (LANES, LANES), 1) // HEAD
    return jnp.where(r == c, 1.0, 0.0).astype(BF16)


def _group_sum(x, ones):
    def tile_sum(t):
        hi, lo = _split2(t)
        return _dot(hi, ones) + _dot(lo, ones)

    tiles = [tile_sum(x[:, j * LANES:(j + 1) * LANES]) for j in range(x.shape[1] // LANES)]
    return tiles[0] if len(tiles) == 1 else jnp.concatenate(tiles, axis=1)


def _rmsnorm_kernel(x_ref, g_ref, o_ref):
    x = x_ref[...]
    y = x * lax.rsqrt(jnp.mean(x * x, axis=-1, keepdims=True) + RMS_EPS)
    o_ref[...] = (y * g_ref[...]).astype(o_ref.dtype)


def rmsnorm_bf16(x, g, tm):
    m, d = x.shape
    return pl.pallas_call(
        _rmsnorm_kernel,
        grid=(m // tm,),
        in_specs=[pl.BlockSpec((tm, d), lambda i: (i, 0)), pl.BlockSpec((1, d), lambda i: (0, 0))],
        out_specs=pl.BlockSpec((tm, d), lambda i: (i, 0)),
        out_shape=jax.ShapeDtypeStruct((m, d), BF16),
        compiler_params=_cparams("parallel"),
    )(x, g.reshape(1, d))


def _matmul_kernel(a_ref, w_ref, o_ref, *, act):
    acc = _dot(a_ref[...], w_ref[...])
    if act == "sigmoid":
        acc = _sigmoid(acc)
    o_ref[...] = acc.astype(o_ref.dtype)


def matmul(a, w, out_dtype, tm, tn, act=None):
    m, k = a.shape
    n = w.shape[1]
    return pl.pallas_call(
        functools.partial(_matmul_kernel, act=act),
        grid=(n // tn, m // tm),
        in_specs=[pl.BlockSpec((tm, k), lambda j, i: (i, 0)), pl.BlockSpec((k, tn), lambda j, i: (0, j))],
        out_specs=pl.BlockSpec((tm, tn), lambda j, i: (i, j)),
        out_shape=jax.ShapeDtypeStruct((m, n), out_dtype),
        compiler_params=_cparams("parallel", "parallel"),
    )(a, w)


def _qk_prep_kernel(qk_ref, cos_ref, sin_a_ref, sin_b_ref, gq_ref, gk_ref, q_ref, k_ref):
    ones = _head_block_ones()
    cos, sin_a, sin_b = cos_ref[...], sin_a_ref[...], sin_b_ref[...]
    n_tiles = q_ref.shape[1] // LANES
    for j in range(2 * n_tiles):
        x = qk_ref[:, j * LANES:(j + 1) * LANES]
        ms = _group_sum(x * x, ones) * (1.0 / HEAD)
        is_q = j < n_tiles
        y = x * lax.rsqrt(ms + RMS_EPS) * (gq_ref[...] if is_q else gk_ref[...])
        y = y * cos + pltpu.roll(y, LANES - ROPE_DIM // 2, 1) * sin_a + pltpu.roll(y, ROPE_DIM // 2, 1) * sin_b
        if is_q:
            q_ref[:, j * LANES:(j + 1) * LANES] = (y * HEAD ** -0.5).astype(q_ref.dtype)
        else:
            jj = j - n_tiles
            k_ref[:, jj * LANES:(jj + 1) * LANES] = y.astype(k_ref.dtype)


def qk_prep(qk, cos, sin_a, sin_b, gq, gk, seq, tm):
    m, w2 = qk.shape
    w = w2 // 2
    st = seq // tm
    tab = pl.BlockSpec((tm, LANES), lambda i: (i % st, 0))
    vec = pl.BlockSpec((1, LANES), lambda i: (0, 0))
    return pl.pallas_call(
        _qk_prep_kernel,
        grid=(m // tm,),
        in_specs=[pl.BlockSpec((tm, w2), lambda i: (i, 0)), tab, tab, tab, vec, vec],
        out_specs=[pl.BlockSpec((tm, w), lambda i: (i, 0))] * 2,
        out_shape=[jax.ShapeDtypeStruct((m, w), BF16)] * 2,
        compiler_params=_cparams("parallel"),
    )(qk, cos, sin_a, sin_b, gq, gk)


KEY_CHUNK = 512
MAX_SCORE_BOUND = 40.0


def _attn_kernel(bound_ref, lam_ref, q_ref, k_ref, v_ref, g_ref, o_ref, vaug_ref, *, lam_init):
    lp = lam_ref[...]
    lam = (jnp.exp(jnp.sum(lp[0:1] * lp[1:2], axis=-1, keepdims=True))
           - jnp.exp(jnp.sum(lp[2:3] * lp[3:4], axis=-1, keepdims=True)) + lam_init)
    seq = k_ref.shape[0]
    tq = q_ref.shape[0]
    tk = min(KEY_CHUNK, seq)

    @pl.when(pl.program_id(2) == 0)
    def _():
        vaug_ref[:, :LANES] = v_ref[...]
        vaug_ref[:, LANES:] = jnp.ones((seq, LANES), BF16)

    q = q_ref[...]
    lo = lax.broadcasted_iota(jnp.int32, (1, LANES), 1) < HEAD
    zero = jnp.zeros_like(q)
    q_lo, q_hi = jnp.where(lo, q, zero), jnp.where(lo, zero, q)

    def finish(o2):
        o = o2[:tq] - lam * o2[tq:]
        o = o * lax.rsqrt(jnp.mean(o * o, axis=-1, keepdims=True) + RMS_EPS)
        o_ref[...] = (o * g_ref[...] * (1.0 - lam_init)).astype(o_ref.dtype)

    bound = bound_ref[0, 0]
    safe = bound <= MAX_SCORE_BOUND

    @pl.when(safe)
    def _():
        q2 = jnp.concatenate([q_lo, q_hi], axis=0)
        acc = jnp.zeros((2 * tq, 2 * LANES), F32)
        for c in range(seq // tk):
            s = _dot_nt(q2, k_ref[c * tk:(c + 1) * tk, :])
            acc = acc + _dot(jnp.exp(s - bound).astype(BF16), vaug_ref[c * tk:(c + 1) * tk, :])
        finish(acc[:, :LANES] / acc[:, LANES:LANES + 1])

    @pl.when(jnp.logical_not(safe))
    def _():
        def branch(qm):
            s = _dot_nt(qm, k_ref[...])
            p = jnp.exp(s - jnp.max(s, axis=-1, keepdims=True))
            return _dot(p.astype(BF16), v_ref[...]) / jnp.sum(p, axis=-1, keepdims=True)

        finish(jnp.concatenate([branch(q_lo), branch(q_hi)], axis=0))


def diff_attention(q, k, v, score_bound, lam_params, subln_g, batch, lam_init, tq):
    m, w = q.shape
    seq = m // batch
    nq = seq // tq
    return pl.pallas_call(
        functools.partial(_attn_kernel, lam_init=lam_init),
        grid=(batch, w // LANES, nq),
        in_specs=[pl.BlockSpec(memory_space=pltpu.SMEM),
                  pl.BlockSpec((4, HEAD), lambda b, h, i: (0, 0)),
                  pl.BlockSpec((tq, LANES), lambda b, h, i: (b * nq + i, h)),
                  pl.BlockSpec((seq, LANES), lambda b, h, i: (b, h)),
                  pl.BlockSpec((seq, LANES), lambda b, h, i: (b, h)),
                  pl.BlockSpec((1, LANES), lambda b, h, i: (0, 0))],
        out_specs=pl.BlockSpec((tq, LANES), lambda b, h, i: (b * nq + i, h)),
        out_shape=jax.ShapeDtypeStruct((m, w), BF16),
        scratch_shapes=[pltpu.VMEM((seq, 2 * LANES), BF16)],
        compiler_params=_cparams("parallel", "parallel", "arbitrary"),
    )(score_bound, lam_params, q, k, v, subln_g.reshape(1, LANES))


def _rwkv_prep_kernel(p_ref, prev_ref, next_ref, mu_ref, w0_ref, w2h_ref, w2l_ref, a0_ref, a2_ref,
                      g2_ref, kkw_ref, ka_ref, rk_ref,
                      r_o, v_o, kk_o, kf_o, kb_o, lwf_o, lwb_o, af_o, ab_o, bonus_o, g_o, *, seq_tiles):
    tm = p_ref.shape[0]
    c = r_o.shape[1]
    ti = pl.program_id(0) % seq_tiles
    p = p_ref[...]
    row = lax.broadcasted_iota(jnp.int32, (tm, 1), 0)
    prev_row = jnp.where(ti > 0, prev_ref[7:8, :], 0.0)
    next_row = jnp.where(ti < seq_tiles - 1, next_ref[0:1, :], 0.0)
    prev = jnp.where(row == 0, prev_row, pltpu.roll(p, 1, 0))
    nxt = jnp.where(row == tm - 1, next_row, pltpu.roll(p, tm - 1, 0))
    p = p + mu_ref[...] * (0.5 * (prev + nxt) - p)

    r, k, v = p[:, :c], p[:, c:2 * c], p[:, 2 * c:3 * c]
    wd = p[:, 3 * c:3 * c + LANES]
    ad = p[:, 3 * c + LANES:3 * c + 2 * LANES]
    gd = p[:, 3 * c + 2 * LANES:]
    wl = w0_ref[...] + _dot_f32(jnp.tanh(wd), w2h_ref[...], w2l_ref[...])
    lw = -math.exp(-0.5) * _sigmoid(wl)
    a = _sigmoid(a0_ref[...] + _dot(ad.astype(BF16), a2_ref[...]))
    g_o[...] = _dot(_sigmoid(gd).astype(BF16), g2_ref[...])
    a_f, a_b = a[:, :c], a[:, c:]
    ones = _head_block_ones()
    kk = k * kkw_ref[...]
    kk = kk / jnp.maximum(jnp.sqrt(_group_sum(kk * kk, ones)), 1e-12)
    k_a = ka_ref[...]
    k_f = k * (1.0 + (a_f - 1.0) * k_a)
    k_b = k * (1.0 + (a_b - 1.0) * k_a)
    bonus_o[...] = _group_sum(r * (k_f + k_b) * rk_ref[...], ones) * v
    r_o[...] = r
    v_o[...] = v
    kk_o[...] = kk
    kf_o[...] = k_f
    kb_o[...] = k_b
    lwf_o[...] = lw[:, :c]
    lwb_o[...] = lw[:, c:]
    af_o[...] = a_f
    ab_o[...] = a_b


def rwkv_prep(p, mu, w0, w2, a0, a2, g2, k_k, k_a, r_k, seq, tm, c):
    m, pc = p.shape
    st = seq // tm
    hb = tm // 8
    last = m // 8 - 1
    full = lambda a: pl.BlockSpec(a.shape, lambda i: (0, 0))
    row_c = pl.BlockSpec((tm, c), lambda i: (i, 0))
    small = [mu, w0, w2[0], w2[1], a0, a2, g2, k_k, k_a, r_k]
    return pl.pallas_call(
        functools.partial(_rwkv_prep_kernel, seq_tiles=st),
        grid=(m // tm,),
        in_specs=[pl.BlockSpec((tm, pc), lambda i: (i, 0)),
                  pl.BlockSpec((8, pc), lambda i: (jnp.maximum(i * hb - 1, 0), 0)),
                  pl.BlockSpec((8, pc), lambda i: (jnp.minimum((i + 1) * hb, last), 0))] + [full(a) for a in small],
        out_specs=[row_c] * 11,
        out_shape=[jax.ShapeDtypeStruct((m, c), F32)] * 11,
        compiler_params=_cparams("parallel"),
    )(p, p, p, *small)


def _rwkv_post_kernel(yf_ref, yb_ref, bonus_ref, g_ref, lnw_ref, lnb_ref, o_ref):
    ones = _head_block_ones()
    y = yf_ref[...] + yb_ref[...]
    mean = _group_sum(y, ones) * (1.0 / HEAD)
    yc = y - mean
    var = _group_sum(yc * yc, ones) * (1.0 / HEAD)
    yn = yc * lax.rsqrt(var + GN_EPS) * lnw_ref[...] + lnb_ref[...]
    o_ref[...] = ((yn + bonus_ref[...]) * g_ref[...]).astype(o_ref.dtype)


def rwkv_post(yf, yb, bonus, g, ln_w, ln_b, tm):
    m, c = yf.shape
    row = pl.BlockSpec((tm, c), lambda i: (i, 0))
    vec = pl.BlockSpec((1, c), lambda i: (0, 0))
    return pl.pallas_call(
        _rwkv_post_kernel,
        grid=(m // tm,),
        in_specs=[row] * 4 + [vec] * 2,
        out_specs=row,
        out_shape=jax.ShapeDtypeStruct((m, c), BF16),
        compiler_params=_cparams("parallel"),
    )(yf, yb, bonus, g, ln_w.reshape(1, c), ln_b.reshape(1, c))


MERGE_ROWS = 128


def _merge_kernel(ya_ref, yb_ref, gate_ref, x_ref, wa_ref, wb_ref, wo_ref, fg_ref, wrh_ref, wrl_ref,
                  h_ref, hn_ref, aff_ref):
    d = x_ref.shape[1]
    tm = x_ref.shape[0]
    sub = min(MERGE_ROWS, tm)
    rows = [slice(r * sub, (r + 1) * sub) for r in range(tm // sub)]
    pa = [_dot(ya_ref[r, :], wa_ref[...]) for r in rows]
    pb = [_dot(yb_ref[r, :], wb_ref[...]) for r in rows]
    merged = [(gate_ref[r, :d].astype(F32) * a + gate_ref[r, d:].astype(F32) * b).astype(BF16)
              for r, a, b in zip(rows, pa, pb)]
    hs = [x_ref[r, :] + _dot(mg, wo_ref[...]) for r, mg in zip(rows, merged)]
    for r, h in zip(rows, hs):
        h_ref[r, :] = h
        hn = h * lax.rsqrt(jnp.mean(h * h, axis=-1, keepdims=True) + RMS_EPS) * fg_ref[...]
        hn_ref[r, :] = hn.astype(hn_ref.dtype)
        hn_hi, hn_lo = _split2(hn)
        wr_hi = wrh_ref[...]
        logits = _dot_nt(wr_hi, hn_hi) + _dot_nt(wr_hi, hn_lo) + _dot_nt(wrl_ref[...], hn_hi)
        e = jnp.exp(logits - jnp.max(logits, axis=0, keepdims=True))
        aff_ref[:, r] = e / jnp.sum(e, axis=0, keepdims=True)


def merge_out_router(ya, yb, gates, x, wa, wb, wo, ffn_g, wr_hi, wr_lo, tm):
    m, d = x.shape
    full = lambda a: pl.BlockSpec(a.shape, lambda i: (0, 0), pipeline_mode=pl.Buffered(1))
    row = lambda a: pl.BlockSpec((tm, a.shape[1]), lambda i: (i, 0))
    ne = wr_hi.shape[0]
    return pl.pallas_call(
        _merge_kernel,
        grid=(m // tm,),
        in_specs=[row(ya), row(yb), row(gates), row(x), full(wa), full(wb), full(wo), full(ffn_g),
                  full(wr_hi), full(wr_lo)],
        out_specs=[pl.BlockSpec((tm, d), lambda i: (i, 0)), pl.BlockSpec((tm, d), lambda i: (i, 0)),
                   pl.BlockSpec((ne, tm), lambda i: (0, i))],
        out_shape=[jax.ShapeDtypeStruct((m, d), F32), jax.ShapeDtypeStruct((m, d), BF16),
                   jax.ShapeDtypeStruct((ne, m), F32)],
        compiler_params=_cparams("parallel"),
    )(ya, yb, gates, x, wa, wb, wo, ffn_g, wr_hi, wr_lo)


def _prefix_excl(m_bf16, upper):
    rows, t = m_bf16.shape
    off = jnp.zeros((rows, 1), F32)
    out = []
    for j in range(t // LANES):
        blk = m_bf16[:, j * LANES:(j + 1) * LANES]
        out.append(_dot(blk, upper) + off)
        off = off + jnp.sum(blk.astype(F32), axis=-1, keepdims=True)
    return jnp.concatenate(out, axis=1)


TOKEN_BLOCK = 256
SLOT_WINDOW = 128
SLOT_ALIGN = 16


def _select_kernel(aff_ref, pos_ref, cnt_ref, *, cap):
    aff = aff_ref[...]
    count = lambda pred: jnp.sum(jnp.where(pred, 1.0, 0.0), axis=-1, keepdims=True)
    as_f32 = lambda bits: pltpu.bitcast(jnp.broadcast_to(bits, aff.shape), F32)

    def step(i, thr):
        cand = thr | (jnp.int32(1) << (30 - i))
        return jnp.where(count(aff >= as_f32(cand)) >= cap, cand, thr)

    thr = lax.fori_loop(0, 31, step, jnp.zeros((aff.shape[0], 1), jnp.int32))
    gt = aff >= as_f32(thr + 1)
    eq = (aff >= as_f32(thr)) & jnp.logical_not(gt)
    r = lax.broadcasted_iota(jnp.int32, (LANES, LANES), 0)
    c = lax.broadcasted_iota(jnp.int32, (LANES, LANES), 1)
    upper = jnp.where(r < c, 1.0, 0.0).astype(BF16)
    need = cap - jnp.sum(jnp.where(gt, 1.0, 0.0), axis=-1, keepdims=True)
    eq_rank = _prefix_excl(jnp.where(eq, 1.0, 0.0).astype(BF16), upper)
    sel = gt | (eq & (eq_rank < need))
    sel_bf = jnp.where(sel, 1.0, 0.0).astype(BF16)
    pos = _prefix_excl(sel_bf, upper)
    pos_ref[...] = jnp.where(sel, pos, -1.0).astype(jnp.int32)
    seq = aff.shape[1]
    tok = lax.broadcasted_iota(jnp.int32, (seq, LANES), 0)
    blk = lax.broadcasted_iota(jnp.int32, (seq, LANES), 1)
    before = jnp.where(tok < blk * TOKEN_BLOCK, 1.0, 0.0).astype(BF16)
    cnt_ref[0] = _dot(sel_bf, before).astype(jnp.int32)


def select_slots(aff_t, batch, cap):
    ne, m = aff_t.shape
    seq = m // batch
    assert seq // TOKEN_BLOCK < LANES
    return pl.pallas_call(
        functools.partial(_select_kernel, cap=cap),
        grid=(batch,),
        in_specs=[pl.BlockSpec((ne, seq), lambda b: (0, b))],
        out_specs=[pl.BlockSpec((ne, seq), lambda b: (0, b)), pl.BlockSpec((1, ne, LANES), lambda b: (b, 0, 0))],
        out_shape=[jax.ShapeDtypeStruct((ne, m), jnp.int32), jax.ShapeDtypeStruct((batch, ne, LANES), jnp.int32)],
        compiler_params=_cparams("parallel"),
    )(aff_t)


def _for_each_window(cnt_ref, bases, n_blocks, sw, cap, step):
    def start(x, j):
        return (cnt_ref[bases[x] + j] // SLOT_ALIGN) * SLOT_ALIGN

    def window(x, lo):
        s0 = pl.multiple_of(jnp.minimum(lo, cap - sw), SLOT_ALIGN)
        slot = s0 + lax.broadcasted_iota(jnp.int32, (sw, 1), 0)
        return x, s0, lambda pos_blk: (pos_blk == slot) & (slot >= lo)

    for j in range(n_blocks):
        step(j, [window(x, start(x, j)) for x in range(len(bases))])
    for j in range(n_blocks):
        for x in range(len(bases)):
            lo0 = start(x, j)
            n_win = (cnt_ref[bases[x] + j + 1] - lo0 + sw - 1) // sw

            def extra(k, carry, j=j, x=x, lo0=lo0):
                step(j, [window(x, lo0 + k * sw)])
                return carry

            lax.fori_loop(1, n_win, extra, 0)


GATHER_GROUP = 1
SCATTER_GROUP = 2


def _group_bases(batch_id, group_id, n_groups, group):
    first = (batch_id * n_groups + group_id) * group
    return [(first + x) * LANES for x in range(group)]


def _gather_kernel(cnt_ref, pos_ref, aff_ref, hn_ref, xe_ref, gate_ref, acc_ref, gacc_ref):
    seq = hn_ref.shape[0]
    group, cap, _ = acc_ref.shape
    tb = min(TOKEN_BLOCK, seq)
    sw = min(SLOT_WINDOW, cap)
    bases = _group_bases(pl.program_id(0), pl.program_id(1), pl.num_programs(1), group)
    acc_ref[...] = jnp.zeros_like(acc_ref)
    gacc_ref[...] = jnp.zeros_like(gacc_ref)

    def step(j, windows):
        tok = slice(j * tb, (j + 1) * tb)
        hits = [hit_of(pos_ref[x, :, tok]) for x, _, hit_of in windows]
        onehot = jnp.concatenate([jnp.where(h, 1.0, 0.0).astype(BF16) for h in hits], axis=0)
        rows = _dot(onehot, hn_ref[tok, :])
        for i, ((x, s0, _), h) in enumerate(zip(windows, hits)):
            acc_ref[x, pl.ds(s0, sw), :] += rows[i * sw:(i + 1) * sw]
            gacc_ref[x, pl.ds(s0, sw), :] += jnp.sum(jnp.where(h, aff_ref[x, :, tok], 0.0), axis=-1, keepdims=True)

    _for_each_window(cnt_ref, bases, seq // tb, sw, cap, step)
    xe_ref[:, 0] = acc_ref[...].astype(xe_ref.dtype)
    gate_ref[:, 0] = gacc_ref[...]


def moe_gather(cnt, pos3, aff3, hn, batch, cap):
    ne = pos3.shape[0]
    m, d = hn.shape
    seq = m // batch
    g = GATHER_GROUP
    assert cap % min(SLOT_WINDOW, cap) == 0 and ne % g == 0
    row = pl.BlockSpec((g, 1, seq), lambda b, e, c: (e, 0, b))
    return pl.pallas_call(
        _gather_kernel,
        grid_spec=pltpu.PrefetchScalarGridSpec(
            num_scalar_prefetch=1,
            grid=(batch, ne // g),
            in_specs=[row, row, pl.BlockSpec((seq, d), lambda b, e, c: (b, 0))],
            out_specs=[pl.BlockSpec((g, 1, cap, d), lambda b, e, c: (e, b, 0, 0)),
                       pl.BlockSpec((g, 1, cap, 1), lambda b, e, c: (e, b, 0, 0))],
            scratch_shapes=[pltpu.VMEM((g, cap, d), F32), pltpu.VMEM((g, cap, 1), F32)]),
        out_shape=[jax.ShapeDtypeStruct((ne, batch, cap, d), BF16),
                   jax.ShapeDtypeStruct((ne, batch, cap, 1), F32)],
        compiler_params=_cparams("parallel", "parallel"),
    )(cnt, pos3, aff3, hn)


def _expert_kernel(xe_ref, gate_ref, wg_ref, wu_ref, hid_ref, wg_bf, wu_bf):
    @pl.when(pl.program_id(1) == 0)
    def _():
        wg_bf[...] = wg_ref[0].astype(BF16)
        wu_bf[...] = wu_ref[0].astype(BF16)

    xe = xe_ref[0, 0]
    hg = _dot(xe, wg_bf[...])
    hu = _dot(xe, wu_bf[...])
    hid_ref[0, 0] = (hg * _sigmoid(hg) * hu * gate_ref[0, 0]).astype(hid_ref.dtype)


def moe_experts(xe, gate, wg, wu):
    ne, batch, cap, d = xe.shape
    ff = wg.shape[2]
    return pl.pallas_call(
        _expert_kernel,
        grid=(ne, batch),
        in_specs=[pl.BlockSpec((1, 1, cap, d), lambda e, b: (e, b, 0, 0)),
                  pl.BlockSpec((1, 1, cap, 1), lambda e, b: (e, b, 0, 0)),
                  pl.BlockSpec((1, d, ff), lambda e, b: (e, 0, 0)),
                  pl.BlockSpec((1, d, ff), lambda e, b: (e, 0, 0))],
        out_specs=pl.BlockSpec((1, 1, cap, ff), lambda e, b: (e, b, 0, 0)),
        out_shape=jax.ShapeDtypeStruct((ne, batch, cap, ff), BF16),
        scratch_shapes=[pltpu.VMEM((d, ff), BF16)] * 2,
        compiler_params=_cparams("parallel", "arbitrary"),
    )(xe, gate, wg, wu)


def _down_kernel(hid_ref, wd_ref, ye_ref, wd_bf):
    @pl.when(pl.program_id(1) == 0)
    def _():
        wd_bf[...] = wd_ref[0].astype(BF16)

    ye_ref[0, 0] = _dot(hid_ref[0, 0], wd_bf[...]).astype(ye_ref.dtype)


def moe_down(hid, wd):
    ne, batch, cap, ff = hid.shape
    d = wd.shape[2]
    return pl.pallas_call(
        _down_kernel,
        grid=(ne, batch),
        in_specs=[pl.BlockSpec((1, 1, cap, ff), lambda e, b: (e, b, 0, 0)),
                  pl.BlockSpec((1, ff, d), lambda e, b: (e, 0, 0))],
        out_specs=pl.BlockSpec((1, 1, cap, d), lambda e, b: (e, b, 0, 0)),
        out_shape=jax.ShapeDtypeStruct((ne, batch, cap, d), BF16),
        scratch_shapes=[pltpu.VMEM((ff, d), BF16)],
        compiler_params=_cparams("parallel", "arbitrary"),
    )(hid, wd)


def _scatter_kernel(cnt_ref, pos_ref, ye_ref, h_ref, o_ref):
    e = pl.program_id(2)
    tt = h_ref.shape[0]
    group, _, cap, _ = ye_ref.shape
    tb = min(TOKEN_BLOCK, tt)
    sw = min(SLOT_WINDOW, cap)
    tile_off = pl.program_id(1) * (tt // tb)
    bases = [b + tile_off for b in _group_bases(pl.program_id(0), e, pl.num_programs(2), group)]

    @pl.when(e == 0)
    def _():
        o_ref[...] = h_ref[...]

    def step(j, windows):
        tok = slice(j * tb, (j + 1) * tb)
        onehot = jnp.concatenate([jnp.where(hit_of(pos_ref[x, :, tok]), 1.0, 0.0).astype(BF16)
                                  for x, _, hit_of in windows], axis=0)
        ye = jnp.concatenate([ye_ref[x, 0, pl.ds(s0, sw), :] for x, s0, _ in windows], axis=0)
        o_ref[tok, :] += _dot_tn(onehot, ye)

    _for_each_window(cnt_ref, bases, tt // tb, sw, cap, step)


def moe_scatter(cnt, pos3, ye, h, batch, tt):
    ne, _, cap, d = ye.shape
    m = h.shape[0]
    nt = m // batch // tt
    g = SCATTER_GROUP
    assert ne % g == 0
    return pl.pallas_call(
        _scatter_kernel,
        grid_spec=pltpu.PrefetchScalarGridSpec(
            num_scalar_prefetch=1,
            grid=(batch, nt, ne // g),
            in_specs=[pl.BlockSpec((g, 1, tt), lambda b, i, e, c: (e, 0, b * nt + i)),
                      pl.BlockSpec((g, 1, cap, d), lambda b, i, e, c: (e, b, 0, 0)),
                      pl.BlockSpec((tt, d), lambda b, i, e, c: (b * nt + i, 0))],
            out_specs=pl.BlockSpec((tt, d), lambda b, i, e, c: (b * nt + i, 0))),
        out_shape=jax.ShapeDtypeStruct((m, d), F32),
        compiler_params=_cparams("parallel", "parallel", "arbitrary"),
    )(cnt, pos3, ye, h)


def _rope_tables(seq):
    half = ROPE_DIM // 2
    inv = ROPE_THETA ** (-(jnp.arange(0, ROPE_DIM, 2, dtype=F32) / ROPE_DIM))
    ang = jnp.arange(seq, dtype=F32)[:, None] * inv[None, :]
    cos, sin = jnp.cos(ang), jnp.sin(ang)
    one = jnp.ones((seq, HEAD - ROPE_DIM), F32)
    zero = lambda n: jnp.zeros((seq, n), F32)
    cos_t = jnp.concatenate([cos, cos, one], axis=1)
    sin_a = jnp.concatenate([-sin, zero(HEAD - half)], axis=1)
    sin_b = jnp.concatenate([zero(half), sin, zero(HEAD - ROPE_DIM)], axis=1)
    return [jnp.tile(t, (1, LANES // HEAD)) for t in (cos_t, sin_a, sin_b)]


def _hi_lo(w):
    hi = w.astype(BF16)
    return hi, (w - hi.astype(F32)).astype(BF16)


def _block_diag2(wf, wb):
    z = jnp.zeros_like(wf)
    return jnp.concatenate([jnp.concatenate([wf, z], axis=1), jnp.concatenate([z, wb], axis=1)], axis=0)


def _col_tile(n):
    for t in (1024, 896, 512, 256, 128):
        if n % t == 0:
            return t
    raise ValueError(f"unsupported matmul width {n}")


def kernel(x, attn_norm_g, w_in, q_norm_g, k_norm_g, lambda_q1, lambda_k1, lambda_q2, lambda_k2, subln_g, shift_mu, w0_f, w2_f, w0_b, w2_b, a0_f, a2_f, a0_b, a2_b, g2, k_k, k_a, r_k, ln_x_w, ln_x_b, w_branch_a, w_branch_b, w_out, ffn_norm_g, w_router, w_gate_e, w_up_e, w_down_e):
    batch, seq, d = x.shape
    m = batch * seq
    depth = w_in.shape[0]
    c = w_branch_b.shape[1]
    qk_w = 2 * w_branch_a.shape[1]
    v_w = w_branch_a.shape[1]
    rw_cols = shift_mu.shape[1]
    rw_pad = -(-rw_cols // LANES) * LANES
    ne = w_router.shape[2]
    cap = 2 * seq // ne
    tm = min(256, seq)
    tmm = 512 if m % 512 == 0 else tm
    cos_t, sin_a, sin_b = _rope_tables(seq)
    row = lambda a: a.reshape(1, -1)

    h = x.reshape(m, d)
    for l in range(depth):
        lam_init = 0.8 - 0.6 * math.exp(-0.3 * l)
        w = w_in[l]
        w_qk = w[:, :qk_w].astype(BF16)
        w_v = w[:, qk_w:qk_w + v_w].astype(BF16)
        w_rw = jnp.pad(w[:, qk_w + v_w:qk_w + v_w + rw_cols], ((0, 0), (0, rw_pad - rw_cols))).astype(BF16)
        w_gt = w[:, qk_w + v_w + rw_cols:].astype(BF16)

        hn = rmsnorm_bf16(h, attn_norm_g[l], tm)
        qk = matmul(hn, w_qk, F32, tmm, _col_tile(qk_w))
        v = matmul(hn, w_v, BF16, tmm, _col_tile(v_w))
        p_rw = matmul(hn, w_rw, F32, tmm, _col_tile(rw_pad))
        gates = matmul(hn, w_gt, BF16, tmm, _col_tile(2 * d), act="sigmoid")

        gq = row(jnp.tile(q_norm_g[l], LANES // HEAD))
        gk = row(jnp.tile(k_norm_g[l], LANES // HEAD))
        q, k = qk_prep(qk, cos_t, sin_a, sin_b, gq, gk, seq, tm)
        lam_params = jnp.stack([lambda_q1[l], lambda_k1[l], lambda_q2[l], lambda_k2[l]])
        score_bound = (1.02 * math.sqrt(HEAD) * jnp.max(jnp.abs(q_norm_g[l])) * jnp.max(jnp.abs(k_norm_g[l]))
                       ).astype(F32).reshape(1, 1)
        y_a = diff_attention(q, k, v, score_bound, lam_params, subln_g[l], batch, lam_init, min(512, seq))

        mu = jnp.pad(shift_mu[l], (0, rw_pad - rw_cols)).reshape(1, rw_pad)
        w0 = row(jnp.concatenate([w0_f[l], w0_b[l]]))
        a0 = row(jnp.concatenate([a0_f[l], a0_b[l]]))
        w2 = _hi_lo(_block_diag2(w2_f[l], w2_b[l]))
        a2 = _block_diag2(a2_f[l], a2_b[l]).astype(BF16)
        g_rows = rw_pad - 3 * c - 2 * LANES
        g2p = jnp.pad(g2[l], ((0, g_rows - g2.shape[1]), (0, 0))).astype(BF16)
        r, vv, kk, k_f, k_b, lw_f, lw_b, a_f, a_b, bonus, g = rwkv_prep(
            p_rw, mu, w0, w2, a0, a2, g2p, row(k_k[l]), row(k_a[l]), row(r_k[l]), seq, tm, c)
        y_f, y_bk = rwkv_scan(r, vv, kk, lw_f, k_f, a_f, lw_b, k_b, a_b, batch)
        y_b = rwkv_post(y_f, y_bk, bonus, g, ln_x_w[l], ln_x_b[l], tm)

        wr_hi, wr_lo = _hi_lo(w_router[l].T)
        h2, hn2, aff_t = merge_out_router(
            y_a, y_b, gates, h, w_branch_a[l].astype(BF16), w_branch_b[l].astype(BF16), w_out[l].astype(BF16),
            row(ffn_norm_g[l]), wr_hi, wr_lo, tm)

        pos, cnt = select_slots(aff_t, batch, cap)
        pos3 = pos.reshape(ne, 1, m)
        cnt = cnt.reshape(-1)
        xe, gate = moe_gather(cnt, pos3, aff_t.reshape(ne, 1, m), hn2, batch, cap)
        hid = moe_experts(xe, gate, w_gate_e[l], w_up_e[l])
        ye = moe_down(hid, w_down_e[l])
        h = moe_scatter(cnt, pos3, ye, h2, batch, min(1024, seq))
    return h.reshape(batch, seq, d)
```

```python
import functools
import math

import jax
import jax.numpy as jnp
from jax import lax
from jax.experimental import pallas as pl
from jax.experimental.pallas import tpu as pltpu

F32 = jnp.float32
BF16 = jnp.bfloat16

LANES = 128
HEAD = 64
CHUNK = 64
SCAN_CHUNKS = 2
RMS_EPS = 1e-6
GN_EPS = 64e-5
ROPE_THETA = 500000.0
ROPE_DIM = 16
VMEM_LIMIT = 56 * 1024 * 1024


def _cparams(*sem):
    return pltpu.CompilerParams(dimension_semantics=sem, vmem_limit_bytes=VMEM_LIMIT)


def _dot(a, b):
    return jnp.dot(a, b, preferred_element_type=F32)


def _dot_nt(a, b):
    return lax.dot_general(a, b, (((1,), (1,)), ((), ())), preferred_element_type=F32)


def _dot_tn(a, b):
    return lax.dot_general(a, b, (((0,), (0,)), ((), ())), preferred_element_type=F32)


def _split3(x):
    h1 = x.astype(BF16)
    r1 = x - h1.astype(F32)
    h2 = r1.astype(BF16)
    h3 = (r1 - h2.astype(F32)).astype(BF16)
    return h1, h2, h3


def _dot_exact_lhs(w_bf16, x):
    h1, h2, h3 = _split3(x)
    return _dot(w_bf16, h1) + _dot(w_bf16, h2) + _dot(w_bf16, h3)


def _scan_chains(r_ref, v_ref, kk_ref, lw_ref, k_ref, a_ref, y_ref, d, reverse, rows):
    L = CHUNK
    n_pairs = r_ref.shape[-1] // LANES
    t_i = lax.broadcasted_iota(jnp.int32, (L, L), 0)
    i_i = lax.broadcasted_iota(jnp.int32, (L, L), 1)
    tri = jnp.where((i_i >= t_i) if reverse else (i_i <= t_i), 1.0, 0.0).astype(BF16)

    t2 = lax.broadcasted_iota(jnp.int32, (L, 2 * LANES), 0)
    i2 = lax.broadcasted_iota(jnp.int32, (L, 2 * LANES), 1) % HEAD
    strict = (i2 > t2) if reverse else (i2 < t2)
    incl = (i2 >= t2) if reverse else (i2 <= t2)
    cum_all = _dot_exact_lhs(tri, lw_ref[rows, :])
    yield None
    last = 0 if reverse else L - 1
    for p in range(n_pairs):
        sl = slice(p * LANES, (p + 1) * LANES)
        cum = cum_all[:, sl]
        lw = lw_ref[rows, sl]
        tot = cum[last:last + 1, :]
        winv = jnp.exp(-cum)
        wrem = jnp.exp(tot - cum)
        kk = kk_ref[rows, sl]
        b = kk * a_ref[rows, sl]
        k = k_ref[rows, sl]
        rt = (r_ref[rows, sl] * jnp.exp(cum)).astype(BF16)
        at = (-kk * jnp.exp(cum - lw)).astype(BF16)
        yield dict(
            d=d, p=p, sl=sl, rows=rows, y_ref=y_ref, strict=strict, incl=incl,
            ar=jnp.concatenate([at, rt], axis=0), bt=(b * winv).astype(BF16), kt=(k * winv).astype(BF16),
            v=v_ref[rows, sl].astype(BF16), bh=(b * wrem).astype(BF16), kh=(k * wrem).astype(BF16),
            w_tot=jnp.exp(tot))


def _rwkv_scan_kernel(r_f, v_f, kk_f, lw_f, k_f, a_f, r_b, v_b, kk_b, lw_b, k_b, a_b, yf_ref, yb_ref, s_ref):
    @pl.when(pl.program_id(1) == 0)
    def _():
        s_ref[...] = jnp.zeros_like(s_ref)

    L = CHUNK
    n_sub = r_f.shape[0] // L
    def prep(i):
        fwd_rows = slice(i * L, (i + 1) * L)
        bwd_rows = slice((n_sub - 1 - i) * L, (n_sub - i) * L)
        yield from _scan_chains(r_f, v_f, kk_f, lw_f, k_f, a_f, yf_ref, 0, False, fwd_rows)
        yield from _scan_chains(r_b, v_b, kk_b, lw_b, k_b, a_b, yb_ref, 1, True, bwd_rows)

    cur = [c for c in prep(0) if c is not None]
    s = [s_ref[c["d"], c["p"]] for c in cur]
    for i in range(n_sub):
        pieces = prep(i + 1) if i + 1 < n_sub else iter(())
        nxt = []

        def fill(n_pieces):
            for _ in range(n_pieces):
                c = next(pieces, None)
                if c is not None:
                    nxt.append(c)

        s = _scan_stages(cur, s, fill)
        for c in pieces:
            if c is not None:
                nxt.append(c)
        cur = nxt or cur
    for c, si in zip(cur, s):
        s_ref[c["d"], c["p"]] = si


def _scan_stages(cs, s, fill):
    L = CHUNK
    lo = lax.broadcasted_iota(jnp.int32, (1, LANES), 1) < HEAD
    rr = lax.broadcasted_iota(jnp.int32, (LANES, LANES), 0) < HEAD
    cc = lax.broadcasted_iota(jnp.int32, (LANES, LANES), 1) < HEAD
    same_head = rr == cc
    cat = lambda *xs: jnp.concatenate(xs, axis=0)
    zero = jnp.zeros((), BF16)

    def bd(t):
        t = t.astype(BF16)
        return cat(jnp.where(lo, t, zero), jnp.where(lo, zero, t))

    g = [_dot_nt(c["ar"], cat(bd(c["bt"]), bd(c["kt"]))) for c in cs]
    m = [jnp.where(c["strict"], gi[:L], 0.0).astype(BF16) for c, gi in zip(cs, g)]
    n = [jnp.where(c["incl"], gi[L:], 0.0).astype(BF16) for c, gi in zip(cs, g)]
    fill(2)
    ars = [_dot_nt(c["ar"], si.astype(BF16)) for c, si in zip(cs, s)]
    vbd = [bd(c["v"]) for c in cs]
    fill(2)
    x = [a[:L] + _dot(mi[:, LANES:], vb) for a, mi, vb in zip(ars, m, vbd)]
    pw = [mi[:, :LANES] for mi in m]
    fill(2)
    n_sq = int(math.log2(L))
    for j in range(n_sq):
        if j + 1 < n_sq:
            px = [_dot(pi, jnp.concatenate([bd(xi), bd(pi)], axis=1)) for xi, pi in zip(x, pw)]
            x = [xi + pxi[:, :LANES] for xi, pxi in zip(x, px)]
            pw = [pxi[:, LANES:].astype(BF16) for pxi in px]
        else:
            x = [xi + _dot(pi, bd(xi)) for xi, pi in zip(x, pw)]
        fill(2)
    y = [a[L:] + _dot(ni, cat(bd(xi), vb)) for a, ni, xi, vb in zip(ars, n, x, vbd)]
    for c, yi in zip(cs, y):
        c["y_ref"][c["rows"], c["sl"]] = yi
    ds = [_dot_tn(cat(xi.astype(BF16), c["v"]), cat(c["bh"], c["kh"])) for c, xi in zip(cs, x)]
    return [si * c["w_tot"] + jnp.where(same_head, di, 0.0) for c, si, di in zip(cs, s, ds)]


def rwkv_scan(r, v, kk, lw_f, k_f, a_f, lw_b, k_b, a_b, batch):
    m, c = r.shape
    rows = SCAN_CHUNKS * CHUNK
    nc = m // batch // rows
    fwd = pl.BlockSpec((rows, c), lambda bi, ci: (bi * nc + ci, 0))
    bwd = pl.BlockSpec((rows, c), lambda bi, ci: (bi * nc + nc - 1 - ci, 0))
    return pl.pallas_call(
        _rwkv_scan_kernel,
        grid=(batch, nc),
        in_specs=[fwd] * 6 + [bwd] * 6,
        out_specs=[fwd, bwd],
        out_shape=[jax.ShapeDtypeStruct((m, c), F32)] * 2,
        scratch_shapes=[pltpu.VMEM((2, c // LANES, LANES, LANES), F32)],
        compiler_params=_cparams("arbitrary", "arbitrary"),
    )(r, v, kk, lw_f, k_f, a_f, r, v, kk, lw_b, k_b, a_b)


def _sigmoid(x):
    return 1.0 / (1.0 + jnp.exp(-x))


def _split2(x):
    hi = x.astype(BF16)
    return hi, (x - hi.astype(F32)).astype(BF16)


def _dot_f32(a, w_hi, w_lo):
    a_hi, a_lo = _split2(a)
    return _dot(a_hi, w_hi) + _dot(a_lo, w_hi) + _dot(a_hi, w_lo)


def _head_block_ones():
    r = lax.broadcasted_iota(jnp.int32, (LANES, LANES), 0) // HEAD
    c = lax.broadcasted_iota(jnp.int32, (LANES, LANES), 1) // HEAD
    return jnp.where(r == c, 1.0, 0.0).astype(BF16)


def _group_sum(x, ones):
    def tile_sum(t):
        hi, lo = _split2(t)
        return _dot(hi, ones) + _dot(lo, ones)

    tiles = [tile_sum(x[:, j * LANES:(j + 1) * LANES]) for j in range(x.shape[1] // LANES)]
    return tiles[0] if len(tiles) == 1 else jnp.concatenate(tiles, axis=1)


def _rmsnorm_kernel(x_ref, g_ref, o_ref):
    x = x_ref[...]
    y = x * lax.rsqrt(jnp.mean(x * x, axis=-1, keepdims=True) + RMS_EPS)
    o_ref[...] = (y * g_ref[...]).astype(o_ref.dtype)


def rmsnorm_bf16(x, g, tm):
    m, d = x.shape
    return pl.pallas_call(
        _rmsnorm_kernel,
        grid=(m // tm,),
        in_specs=[pl.BlockSpec((tm, d), lambda i: (i, 0)), pl.BlockSpec((1, d), lambda i: (0, 0))],
        out_specs=pl.BlockSpec((tm, d), lambda i: (i, 0)),
        out_shape=jax.ShapeDtypeStruct((m, d), BF16),
        compiler_params=_cparams("parallel"),
    )(x, g.reshape(1, d))


def _split_kernel(w_ref, qk_ref, v_ref, rw_ref, gt_ref, *, widths):
    outs = (qk_ref, v_ref, rw_ref, gt_ref)
    start = 0
    for o_ref, width in zip(outs, widths):
        part = w_ref[0, :, start:start + width].astype(o_ref.dtype)
        if o_ref.shape[1] > width:
            part = jnp.concatenate([part, jnp.zeros((part.shape[0], o_ref.shape[1] - width), o_ref.dtype)], axis=1)
        o_ref[...] = part
        start += width


def split_in_proj(w_in, layer, widths, rw_pad):
    _, k, n = w_in.shape
    tk = 256 if k % 256 == 0 else k
    out_w = (widths[0], widths[1], rw_pad, widths[3])
    return pl.pallas_call(
        functools.partial(_split_kernel, widths=widths),
        grid=(k // tk,),
        in_specs=[pl.BlockSpec((1, tk, n), lambda i: (layer, i, 0))],
        out_specs=[pl.BlockSpec((tk, w), lambda i: (i, 0)) for w in out_w],
        out_shape=[jax.ShapeDtypeStruct((k, w), BF16) for w in out_w],
        compiler_params=_cparams("parallel"),
    )(w_in)


def _matmul_kernel(a_ref, w_ref, o_ref, *, act):
    acc = _dot(a_ref[...], w_ref[...])
    if act == "sigmoid":
        acc = _sigmoid(acc)
    o_ref[...] = acc.astype(o_ref.dtype)


def matmul(a, w, out_dtype, tm, tn, act=None):
    m, k = a.shape
    n = w.shape[1]
    return pl.pallas_call(
        functools.partial(_matmul_kernel, act=act),
        grid=(n // tn, m // tm),
        in_specs=[pl.BlockSpec((tm, k), lambda j, i: (i, 0)), pl.BlockSpec((k, tn), lambda j, i: (0, j))],
        out_specs=pl.BlockSpec((tm, tn), lambda j, i: (i, j)),
        out_shape=jax.ShapeDtypeStruct((m, n), out_dtype),
        compiler_params=_cparams("parallel", "parallel"),
    )(a, w)


def _qk_prep_kernel(qk_ref, cos_ref, sin_a_ref, sin_b_ref, gq_ref, gk_ref, q_ref, k_ref):
    ones = _head_block_ones()
    cos, sin_a, sin_b = cos_ref[...], sin_a_ref[...], sin_b_ref[...]
    n_tiles = q_ref.shape[1] // LANES
    for j in range(2 * n_tiles):
        x = qk_ref[:, j * LANES:(j + 1) * LANES]
        ms = _group_sum(x * x, ones) * (1.0 / HEAD)
        is_q = j < n_tiles
        y = x * lax.rsqrt(ms + RMS_EPS) * (gq_ref[...] if is_q else gk_ref[...])
        y = y * cos + pltpu.roll(y, LANES - ROPE_DIM // 2, 1) * sin_a + pltpu.roll(y, ROPE_DIM // 2, 1) * sin_b
        if is_q:
            q_ref[:, j * LANES:(j + 1) * LANES] = (y * HEAD ** -0.5).astype(q_ref.dtype)
        else:
            jj = j - n_tiles
            k_ref[:, jj * LANES:(jj + 1) * LANES] = y.astype(k_ref.dtype)


def qk_prep(qk, cos, sin_a, sin_b, gq, gk, seq, tm):
    m, w2 = qk.shape
    w = w2 // 2
    st = seq // tm
    tab = pl.BlockSpec((tm, LANES), lambda i: (i % st, 0))
    vec = pl.BlockSpec((1, LANES), lambda i: (0, 0))
    return pl.pallas_call(
        _qk_prep_kernel,
        grid=(m // tm,),
        in_specs=[pl.BlockSpec((tm, w2), lambda i: (i, 0)), tab, tab, tab, vec, vec],
        out_specs=[pl.BlockSpec((tm, w), lambda i: (i, 0))] * 2,
        out_shape=[jax.ShapeDtypeStruct((m, w), BF16)] * 2,
        compiler_params=_cparams("parallel"),
    )(qk, cos, sin_a, sin_b, gq, gk)


KEY_CHUNK = 512
MAX_SCORE_BOUND = 40.0


def _attn_kernel(bound_ref, lam_ref, q_ref, k_ref, v_ref, g_ref, o_ref, vaug_ref, *, lam_init):
    lp = lam_ref[...]
    lam = (jnp.exp(jnp.sum(lp[0:1] * lp[1:2], axis=-1, keepdims=True))
           - jnp.exp(jnp.sum(lp[2:3] * lp[3:4], axis=-1, keepdims=True)) + lam_init)
    seq = k_ref.shape[0]
    tq = q_ref.shape[0]
    tk = min(KEY_CHUNK, seq)

    @pl.when(pl.program_id(2) == 0)
    def _():
        vaug_ref[:, :LANES] = v_ref[...]
        vaug_ref[:, LANES:] = jnp.ones((seq, LANES), BF16)

    q = q_ref[...]
    lo = lax.broadcasted_iota(jnp.int32, (1, LANES), 1) < HEAD
    zero = jnp.zeros_like(q)
    q_lo, q_hi = jnp.where(lo, q, zero), jnp.where(lo, zero, q)

    def finish(o2):
        o = o2[:tq] - lam * o2[tq:]
        o = o * lax.rsqrt(jnp.mean(o * o, axis=-1, keepdims=True) + RMS_EPS)
        o_ref[...] = (o * g_ref[...] * (1.0 - lam_init)).astype(o_ref.dtype)

    bound = bound_ref[0, 0]
    safe = bound <= MAX_SCORE_BOUND

    @pl.when(safe)
    def _():
        q2 = jnp.concatenate([q_lo, q_hi], axis=0)
        acc = jnp.zeros((2 * tq, 2 * LANES), F32)
        for c in range(seq // tk):
            s = _dot_nt(q2, k_ref[c * tk:(c + 1) * tk, :])
            acc = acc + _dot(jnp.exp(s - bound).astype(BF16), vaug_ref[c * tk:(c + 1) * tk, :])
        finish(acc[:, :LANES] / acc[:, LANES:LANES + 1])

    @pl.when(jnp.logical_not(safe))
    def _():
        def branch(qm):
            s = _dot_nt(qm, k_ref[...])
            p = jnp.exp(s - jnp.max(s, axis=-1, keepdims=True))
            return _dot(p.astype(BF16), v_ref[...]) / jnp.sum(p, axis=-1, keepdims=True)

        finish(jnp.concatenate([branch(q_lo), branch(q_hi)], axis=0))


def diff_attention(q, k, v, score_bound, lam_params, subln_g, batch, lam_init, tq):
    m, w = q.shape
    seq = m // batch
    nq = seq // tq
    return pl.pallas_call(
        functools.partial(_attn_kernel, lam_init=lam_init),
        grid=(batch, w // LANES, nq),
        in_specs=[pl.BlockSpec(memory_space=pltpu.SMEM),
                  pl.BlockSpec((4, HEAD), lambda b, h, i: (0, 0)),
                  pl.BlockSpec((tq, LANES), lambda b, h, i: (b * nq + i, h)),
                  pl.BlockSpec((seq, LANES), lambda b, h, i: (b, h)),
                  pl.BlockSpec((seq, LANES), lambda b, h, i: (b, h)),
                  pl.BlockSpec((1, LANES), lambda b, h, i: (0, 0))],
        out_specs=pl.BlockSpec((tq, LANES), lambda b, h, i: (b * nq + i, h)),
        out_shape=jax.ShapeDtypeStruct((m, w), BF16),
        scratch_shapes=[pltpu.VMEM((seq, 2 * LANES), BF16)],
        compiler_params=_cparams("parallel", "parallel", "arbitrary"),
    )(score_bound, lam_params, q, k, v, subln_g.reshape(1, LANES))


def _rwkv_prep_kernel(p_ref, prev_ref, next_ref, mu_ref, w0_ref, w2h_ref, w2l_ref, a0_ref, a2_ref,
                      g2_ref, kkw_ref, ka_ref, rk_ref,
                      r_o, v_o, kk_o, kf_o, kb_o, lwf_o, lwb_o, af_o, ab_o, bonus_o, g_o, *, seq_tiles):
    tm = p_ref.shape[0]
    c = r_o.shape[1]
    ti = pl.program_id(0) % seq_tiles
    p = p_ref[...]
    row = lax.broadcasted_iota(jnp.int32, (tm, 1), 0)
    prev_row = jnp.where(ti > 0, prev_ref[7:8, :], 0.0)
    next_row = jnp.where(ti < seq_tiles - 1, next_ref[0:1, :], 0.0)
    prev = jnp.where(row == 0, prev_row, pltpu.roll(p, 1, 0))
    nxt = jnp.where(row == tm - 1, next_row, pltpu.roll(p, tm - 1, 0))
    p = p + mu_ref[...] * (0.5 * (prev + nxt) - p)

    r, k, v = p[:, :c], p[:, c:2 * c], p[:, 2 * c:3 * c]
    wd = p[:, 3 * c:3 * c + LANES]
    ad = p[:, 3 * c + LANES:3 * c + 2 * LANES]
    gd = p[:, 3 * c + 2 * LANES:]
    wl = w0_ref[...] + _dot_f32(jnp.tanh(wd), w2h_ref[...], w2l_ref[...])
    lw = -math.exp(-0.5) * _sigmoid(wl)
    a = _sigmoid(a0_ref[...] + _dot(ad.astype(BF16), a2_ref[...]))
    g_o[...] = _dot(_sigmoid(gd).astype(BF16), g2_ref[...])
    a_f, a_b = a[:, :c], a[:, c:]
    ones = _head_block_ones()
    kk = k * kkw_ref[...]
    kk = kk / jnp.maximum(jnp.sqrt(_group_sum(kk * kk, ones)), 1e-12)
    k_a = ka_ref[...]
    k_f = k * (1.0 + (a_f - 1.0) * k_a)
    k_b = k * (1.0 + (a_b - 1.0) * k_a)
    bonus_o[...] = _group_sum(r * (k_f + k_b) * rk_ref[...], ones) * v
    r_o[...] = r
    v_o[...] = v
    kk_o[...] = kk
    kf_o[...] = k_f
    kb_o[...] = k_b
    lwf_o[...] = lw[:, :c]
    lwb_o[...] = lw[:, c:]
    af_o[...] = a_f
    ab_o[...] = a_b


def rwkv_prep(p, mu, w0, w2, a0, a2, g2, k_k, k_a, r_k, seq, tm, c):
    m, pc = p.shape
    st = seq // tm
    hb = tm // 8
    last = m // 8 - 1
    full = lambda a: pl.BlockSpec(a.shape, lambda i: (0, 0))
    row_c = pl.BlockSpec((tm, c), lambda i: (i, 0))
    small = [mu, w0, w2[0], w2[1], a0, a2, g2, k_k, k_a, r_k]
    return pl.pallas_call(
        functools.partial(_rwkv_prep_kernel, seq_tiles=st),
        grid=(m // tm,),
        in_specs=[pl.BlockSpec((tm, pc), lambda i: (i, 0)),
                  pl.BlockSpec((8, pc), lambda i: (jnp.maximum(i * hb - 1, 0), 0)),
                  pl.BlockSpec((8, pc), lambda i: (jnp.minimum((i + 1) * hb, last), 0))] + [full(a) for a in small],
        out_specs=[row_c] * 11,
        out_shape=[jax.ShapeDtypeStruct((m, c), F32)] * 11,
        compiler_params=_cparams("parallel"),
    )(p, p, p, *small)


def _rwkv_post_kernel(yf_ref, yb_ref, bonus_ref, g_ref, lnw_ref, lnb_ref, o_ref):
    ones = _head_block_ones()
    y = yf_ref[...] + yb_ref[...]
    mean = _group_sum(y, ones) * (1.0 / HEAD)
    yc = y - mean
    var = _group_sum(yc * yc, ones) * (1.0 / HEAD)
    yn = yc * lax.rsqrt(var + GN_EPS) * lnw_ref[...] + lnb_ref[...]
    o_ref[...] = ((yn + bonus_ref[...]) * g_ref[...]).astype(o_ref.dtype)


def rwkv_post(yf, yb, bonus, g, ln_w, ln_b, tm):
    m, c = yf.shape
    row = pl.BlockSpec((tm, c), lambda i: (i, 0))
    vec = pl.BlockSpec((1, c), lambda i: (0, 0))
    return pl.pallas_call(
        _rwkv_post_kernel,
        grid=(m // tm,),
        in_specs=[row] * 4 + [vec] * 2,
        out_specs=row,
        out_shape=jax.ShapeDtypeStruct((m, c), BF16),
        compiler_params=_cparams("parallel"),
    )(yf, yb, bonus, g, ln_w.reshape(1, c), ln_b.reshape(1, c))


MERGE_ROWS = 128


def _merge_kernel(ya_ref, yb_ref, gate_ref, x_ref, wa_ref, wb_ref, wo_ref, fg_ref, wrh_ref, wrl_ref,
                  h_ref, hn_ref, aff_ref):
    d = x_ref.shape[1]
    tm = x_ref.shape[0]
    sub = min(MERGE_ROWS, tm)
    rows = [slice(r * sub, (r + 1) * sub) for r in range(tm // sub)]
    pa = [_dot(ya_ref[r, :], wa_ref[...]) for r in rows]
    pb = [_dot(yb_ref[r, :], wb_ref[...]) for r in rows]
    merged = [(gate_ref[r, :d].astype(F32) * a + gate_ref[r, d:].astype(F32) * b).astype(BF16)
              for r, a, b in zip(rows, pa, pb)]
    hs = [x_ref[r, :] + _dot(mg, wo_ref[...]) for r, mg in zip(rows, merged)]
    for r, h in zip(rows, hs):
        h_ref[r, :] = h
        hn = h * lax.rsqrt(jnp.mean(h * h, axis=-1, keepdims=True) + RMS_EPS) * fg_ref[...]
        hn_ref[r, :] = hn.astype(hn_ref.dtype)
        hn_hi, hn_lo = _split2(hn)
        wr_hi = wrh_ref[...]
        logits = _dot_nt(wr_hi, hn_hi) + _dot_nt(wr_hi, hn_lo) + _dot_nt(wrl_ref[...], hn_hi)
        e = jnp.exp(logits - jnp.max(logits, axis=0, keepdims=True))
        aff_ref[:, r] = e / jnp.sum(e, axis=0, keepdims=True)


def merge_out_router(ya, yb, gates, x, wa, wb, wo, ffn_g, wr_hi, wr_lo, tm):
    m, d = x.shape
    full = lambda a: pl.BlockSpec(a.shape, lambda i: (0, 0), pipeline_mode=pl.Buffered(1))
    row = lambda a: pl.BlockSpec((tm, a.shape[1]), lambda i: (i, 0))
    ne = wr_hi.shape[0]
    return pl.pallas_call(
        _merge_kernel,
        grid=(m // tm,),
        in_specs=[row(ya), row(yb), row(gates), row(x), full(wa), full(wb), full(wo), full(ffn_g),
                  full(wr_hi), full(wr_lo)],
        out_specs=[pl.BlockSpec((tm, d), lambda i: (i, 0)), pl.BlockSpec((tm, d), lambda i: (i, 0)),
                   pl.BlockSpec((ne, tm), lambda i: (0, i))],
        out_shape=[jax.ShapeDtypeStruct((m, d), F32), jax.ShapeDtypeStruct((m, d), BF16),
                   jax.ShapeDtypeStruct((ne, m), F32)],
        compiler_params=_cparams("parallel"),
    )(ya, yb, gates, x, wa, wb, wo, ffn_g, wr_hi, wr_lo)


def _prefix_excl(m_bf16, upper):
    rows, t = m_bf16.shape
    off = jnp.zeros((rows, 1), F32)
    out = []
    for j in range(t // LANES):
        blk = m_bf16[:, j * LANES:(j + 1) * LANES]
        out.append(_dot(blk, upper) + off)
        off = off + jnp.sum(blk.astype(F32), axis=-1, keepdims=True)
    return jnp.concatenate(out, axis=1)


TOKEN_BLOCK = 256
SLOT_WINDOW = 128
SLOT_ALIGN = 16


def _select_kernel(aff_ref, pos_ref, cnt_ref, *, cap):
    aff = aff_ref[...]
    count = lambda pred: jnp.sum(jnp.where(pred, 1.0, 0.0), axis=-1, keepdims=True)
    as_f32 = lambda bits: pltpu.bitcast(jnp.broadcast_to(bits, aff.shape), F32)

    def step(i, thr):
        cand = thr | (jnp.int32(1) << (30 - i))
        return jnp.where(count(aff >= as_f32(cand)) >= cap, cand, thr)

    thr = lax.fori_loop(0, 31, step, jnp.zeros((aff.shape[0], 1), jnp.int32))
    gt = aff >= as_f32(thr + 1)
    eq = (aff >= as_f32(thr)) & jnp.logical_not(gt)
    r = lax.broadcasted_iota(jnp.int32, (LANES, LANES), 0)
    c = lax.broadcasted_iota(jnp.int32, (LANES, LANES), 1)
    upper = jnp.where(r < c, 1.0, 0.0).astype(BF16)
    need = cap - jnp.sum(jnp.where(gt, 1.0, 0.0), axis=-1, keepdims=True)
    eq_rank = _prefix_excl(jnp.where(eq, 1.0, 0.0).astype(BF16), upper)
    sel = gt | (eq & (eq_rank < need))
    sel_bf = jnp.where(sel, 1.0, 0.0).astype(BF16)
    pos = _prefix_excl(sel_bf, upper)
    pos_ref[...] = jnp.where(sel, pos, -1.0).astype(jnp.int32)
    seq = aff.shape[1]
    tok = lax.broadcasted_iota(jnp.int32, (seq, LANES), 0)
    blk = lax.broadcasted_iota(jnp.int32, (seq, LANES), 1)
    before = jnp.where(tok < blk * TOKEN_BLOCK, 1.0, 0.0).astype(BF16)
    cnt_ref[0] = _dot(sel_bf, before).astype(jnp.int32)


def select_slots(aff_t, batch, cap):
    ne, m = aff_t.shape
    seq = m // batch
    assert seq // TOKEN_BLOCK < LANES
    return pl.pallas_call(
        functools.partial(_select_kernel, cap=cap),
        grid=(batch,),
        in_specs=[pl.BlockSpec((ne, seq), lambda b: (0, b))],
        out_specs=[pl.BlockSpec((ne, seq), lambda b: (0, b)), pl.BlockSpec((1, ne, LANES), lambda b: (b, 0, 0))],
        out_shape=[jax.ShapeDtypeStruct((ne, m), jnp.int32), jax.ShapeDtypeStruct((batch, ne, LANES), jnp.int32)],
        compiler_params=_cparams("parallel"),
    )(aff_t)


def _for_each_window(cnt_ref, bases, n_blocks, sw, cap, step):
    def start(x, j):
        return (cnt_ref[bases[x] + j] // SLOT_ALIGN) * SLOT_ALIGN

    def window(x, lo):
        s0 = pl.multiple_of(jnp.minimum(lo, cap - sw), SLOT_ALIGN)
        slot = s0 + lax.broadcasted_iota(jnp.int32, (sw, 1), 0)
        return x, s0, lambda pos_blk: (pos_blk == slot) & (slot >= lo)

    for j in range(n_blocks):
        step(j, [window(x, start(x, j)) for x in range(len(bases))])
    for j in range(n_blocks):
        for x in range(len(bases)):
            lo0 = start(x, j)
            n_win = (cnt_ref[bases[x] + j + 1] - lo0 + sw - 1) // sw

            def extra(k, carry, j=j, x=x, lo0=lo0):
                step(j, [window(x, lo0 + k * sw)])
                return carry

            lax.fori_loop(1, n_win, extra, 0)


GATHER_GROUP = 1
SCATTER_GROUP = 2


def _group_bases(batch_id, group_id, n_groups, group):
    first = (batch_id * n_groups + group_id) * group
    return [(first + x) * LANES for x in range(group)]


def _gather_kernel(cnt_ref, pos_ref, aff_ref, hn_ref, xe_ref, gate_ref, acc_ref, gacc_ref):
    seq = hn_ref.shape[0]
    group, cap, _ = acc_ref.shape
    tb = min(TOKEN_BLOCK, seq)
    sw = min(SLOT_WINDOW, cap)
    bases = _group_bases(pl.program_id(0), pl.program_id(1), pl.num_programs(1), group)
    acc_ref[...] = jnp.zeros_like(acc_ref)
    gacc_ref[...] = jnp.zeros_like(gacc_ref)

    def step(j, windows):
        tok = slice(j * tb, (j + 1) * tb)
        hits = [hit_of(pos_ref[x, :, tok]) for x, _, hit_of in windows]
        onehot = jnp.concatenate([jnp.where(h, 1.0, 0.0).astype(BF16) for h in hits], axis=0)
        rows = _dot(onehot, hn_ref[tok, :])
        for i, ((x, s0, _), h) in enumerate(zip(windows, hits)):
            acc_ref[x, pl.ds(s0, sw), :] += rows[i * sw:(i + 1) * sw]
            gacc_ref[x, pl.ds(s0, sw), :] += jnp.sum(jnp.where(h, aff_ref[x, :, tok], 0.0), axis=-1, keepdims=True)

    _for_each_window(cnt_ref, bases, seq // tb, sw, cap, step)
    xe_ref[:, 0] = acc_ref[...].astype(xe_ref.dtype)
    gate_ref[:, 0] = gacc_ref[...]


def moe_gather(cnt, pos3, aff3, hn, batch, cap):
    ne = pos3.shape[0]
    m, d = hn.shape
    seq = m // batch
    g = GATHER_GROUP
    assert cap % min(SLOT_WINDOW, cap) == 0 and ne % g == 0
    row = pl.BlockSpec((g, 1, seq), lambda b, e, c: (e, 0, b))
    return pl.pallas_call(
        _gather_kernel,
        grid_spec=pltpu.PrefetchScalarGridSpec(
            num_scalar_prefetch=1,
            grid=(batch, ne // g),
            in_specs=[row, row, pl.BlockSpec((seq, d), lambda b, e, c: (b, 0))],
            out_specs=[pl.BlockSpec((g, 1, cap, d), lambda b, e, c: (e, b, 0, 0)),
                       pl.BlockSpec((g, 1, cap, 1), lambda b, e, c: (e, b, 0, 0))],
            scratch_shapes=[pltpu.VMEM((g, cap, d), F32), pltpu.VMEM((g, cap, 1), F32)]),
        out_shape=[jax.ShapeDtypeStruct((ne, batch, cap, d), BF16),
                   jax.ShapeDtypeStruct((ne, batch, cap, 1), F32)],
        compiler_params=_cparams("parallel", "parallel"),
    )(cnt, pos3, aff3, hn)


def _expert_kernel(xe_ref, gate_ref, wg_ref, wu_ref, hid_ref, wg_bf, wu_bf):
    @pl.when(pl.program_id(1) == 0)
    def _():
        wg_bf[...] = wg_ref[0].astype(BF16)
        wu_bf[...] = wu_ref[0].astype(BF16)

    xe = xe_ref[0, 0]
    hg = _dot(xe, wg_bf[...])
    hu = _dot(xe, wu_bf[...])
    hid_ref[0, 0] = (hg * _sigmoid(hg) * hu * gate_ref[0, 0]).astype(hid_ref.dtype)


def moe_experts(xe, gate, wg, wu):
    ne, batch, cap, d = xe.shape
    ff = wg.shape[2]
    return pl.pallas_call(
        _expert_kernel,
        grid=(ne, batch),
        in_specs=[pl.BlockSpec((1, 1, cap, d), lambda e, b: (e, b, 0, 0)),
                  pl.BlockSpec((1, 1, cap, 1), lambda e, b: (e, b, 0, 0)),
                  pl.BlockSpec((1, d, ff), lambda e, b: (e, 0, 0)),
                  pl.BlockSpec((1, d, ff), lambda e, b: (e, 0, 0))],
        out_specs=pl.BlockSpec((1, 1, cap, ff), lambda e, b: (e, b, 0, 0)),
        out_shape=jax.ShapeDtypeStruct((ne, batch, cap, ff), BF16),
        scratch_shapes=[pltpu.VMEM((d, ff), BF16)] * 2,
        compiler_params=_cparams("parallel", "arbitrary"),
    )(xe, gate, wg, wu)


def _down_kernel(hid_ref, wd_ref, ye_ref, wd_bf):
    @pl.when(pl.program_id(1) == 0)
    def _():
        wd_bf[...] = wd_ref[0].astype(BF16)

    ye_ref[0, 0] = _dot(hid_ref[0, 0], wd_bf[...]).astype(ye_ref.dtype)


def moe_down(hid, wd):
    ne, batch, cap, ff = hid.shape
    d = wd.shape[2]
    return pl.pallas_call(
        _down_kernel,
        grid=(ne, batch),
        in_specs=[pl.BlockSpec((1, 1, cap, ff), lambda e, b: (e, b, 0, 0)),
                  pl.BlockSpec((1, ff, d), lambda e, b: (e, 0, 0))],
        out_specs=pl.BlockSpec((1, 1, cap, d), lambda e, b: (e, b, 0, 0)),
        out_shape=jax.ShapeDtypeStruct((ne, batch, cap, d), BF16),
        scratch_shapes=[pltpu.VMEM((ff, d), BF16)],
        compiler_params=_cparams("parallel", "arbitrary"),
    )(hid, wd)


def _scatter_kernel(cnt_ref, pos_ref, ye_ref, h_ref, o_ref):
    e = pl.program_id(2)
    tt = h_ref.shape[0]
    group, _, cap, _ = ye_ref.shape
    tb = min(TOKEN_BLOCK, tt)
    sw = min(SLOT_WINDOW, cap)
    tile_off = pl.program_id(1) * (tt // tb)
    bases = [b + tile_off for b in _group_bases(pl.program_id(0), e, pl.num_programs(2), group)]

    @pl.when(e == 0)
    def _():
        o_ref[...] = h_ref[...]

    def step(j, windows):
        tok = slice(j * tb, (j + 1) * tb)
        onehot = jnp.concatenate([jnp.where(hit_of(pos_ref[x, :, tok]), 1.0, 0.0).astype(BF16)
                                  for x, _, hit_of in windows], axis=0)
        ye = jnp.concatenate([ye_ref[x, 0, pl.ds(s0, sw), :] for x, s0, _ in windows], axis=0)
        o_ref[tok, :] += _dot_tn(onehot, ye)

    _for_each_window(cnt_ref, bases, tt // tb, sw, cap, step)


def moe_scatter(cnt, pos3, ye, h, batch, tt):
    ne, _, cap, d = ye.shape
    m = h.shape[0]
    nt = m // batch // tt
    g = SCATTER_GROUP
    assert ne % g == 0
    return pl.pallas_call(
        _scatter_kernel,
        grid_spec=pltpu.PrefetchScalarGridSpec(
            num_scalar_prefetch=1,
            grid=(batch, nt, ne // g),
            in_specs=[pl.BlockSpec((g, 1, tt), lambda b, i, e, c: (e, 0, b * nt + i)),
                      pl.BlockSpec((g, 1, cap, d), lambda b, i, e, c: (e, b, 0, 0)),
                      pl.BlockSpec((tt, d), lambda b, i, e, c: (b * nt + i, 0))],
            out_specs=pl.BlockSpec((tt, d), lambda b, i, e, c: (b * nt + i, 0))),
        out_shape=jax.ShapeDtypeStruct((m, d), F32),
        compiler_params=_cparams("parallel", "parallel", "arbitrary"),
    )(cnt, pos3, ye, h)


def _rope_tables(seq):
    half = ROPE_DIM // 2
    inv = ROPE_THETA ** (-(jnp.arange(0, ROPE_DIM, 2, dtype=F32) / ROPE_DIM))
    ang = jnp.arange(seq, dtype=F32)[:, None] * inv[None, :]
    cos, sin = jnp.cos(ang), jnp.sin(ang)
    one = jnp.ones((seq, HEAD - ROPE_DIM), F32)
    zero = lambda n: jnp.zeros((seq, n), F32)
    cos_t = jnp.concatenate([cos, cos, one], axis=1)
    sin_a = jnp.concatenate([-sin, zero(HEAD - half)], axis=1)
    sin_b = jnp.concatenate([zero(half), sin, zero(HEAD - ROPE_DIM)], axis=1)
    return [jnp.tile(t, (1, LANES // HEAD)) for t in (cos_t, sin_a, sin_b)]


def _hi_lo(w):
    hi = w.astype(BF16)
    return hi, (w - hi.astype(F32)).astype(BF16)


def _block_diag2(wf, wb):
    z = jnp.zeros_like(wf)
    return jnp.concatenate([jnp.concatenate([wf, z], axis=1), jnp.concatenate([z, wb], axis=1)], axis=0)


def _col_tile(n):
    for t in (1024, 896, 512, 256, 128):
        if n % t == 0:
            return t
    raise ValueError(f"unsupported matmul width {n}")


def kernel(x, attn_norm_g, w_in, q_norm_g, k_norm_g, lambda_q1, lambda_k1, lambda_q2, lambda_k2, subln_g, shift_mu, w0_f, w2_f, w0_b, w2_b, a0_f, a2_f, a0_b, a2_b, g2, k_k, k_a, r_k, ln_x_w, ln_x_b, w_branch_a, w_branch_b, w_out, ffn_norm_g, w_router, w_gate_e, w_up_e, w_down_e):
    batch, seq, d = x.shape
    m = batch * seq
    depth = w_in.shape[0]
    c = w_branch_b.shape[1]
    qk_w = 2 * w_branch_a.shape[1]
    v_w = w_branch_a.shape[1]
    rw_cols = shift_mu.shape[1]
    rw_pad = -(-rw_cols // LANES) * LANES
    ne = w_router.shape[2]
    cap = 2 * seq // ne
    tm = min(256, seq)
    tmm = next((t for t in (1024, 512) if m % t == 0), tm)
    cos_t, sin_a, sin_b = _rope_tables(seq)
    row = lambda a: a.reshape(1, -1)

    h = x.reshape(m, d)
    for l in range(depth):
        lam_init = 0.8 - 0.6 * math.exp(-0.3 * l)
        w_qk, w_v, w_rw, w_gt = split_in_proj(w_in, l, (qk_w, v_w, rw_cols, 2 * d), rw_pad)

        hn = rmsnorm_bf16(h, attn_norm_g[l], tm)
        qk = matmul(hn, w_qk, F32, tmm, _col_tile(qk_w))
        v = matmul(hn, w_v, BF16, tmm, _col_tile(v_w))
        p_rw = matmul(hn, w_rw, F32, tmm, _col_tile(rw_pad))
        gates = matmul(hn, w_gt, BF16, tmm, _col_tile(2 * d), act="sigmoid")

        gq = row(jnp.tile(q_norm_g[l], LANES // HEAD))
        gk = row(jnp.tile(k_norm_g[l], LANES // HEAD))
        q, k = qk_prep(qk, cos_t, sin_a, sin_b, gq, gk, seq, tm)
        lam_params = jnp.stack([lambda_q1[l], lambda_k1[l], lambda_q2[l], lambda_k2[l]])
        score_bound = (1.02 * math.sqrt(HEAD) * jnp.max(jnp.abs(q_norm_g[l])) * jnp.max(jnp.abs(k_norm_g[l]))
                       ).astype(F32).reshape(1, 1)
        y_a = diff_attention(q, k, v, score_bound, lam_params, subln_g[l], batch, lam_init, min(512, seq))

        mu = jnp.pad(shift_mu[l], (0, rw_pad - rw_cols)).reshape(1, rw_pad)
        w0 = row(jnp.concatenate([w0_f[l], w0_b[l]]))
        a0 = row(jnp.concatenate([a0_f[l], a0_b[l]]))
        w2 = _hi_lo(_block_diag2(w2_f[l], w2_b[l]))
        a2 = _block_diag2(a2_f[l], a2_b[l]).astype(BF16)
        g_rows = rw_pad - 3 * c - 2 * LANES
        g2p = jnp.pad(g2[l], ((0, g_rows - g2.shape[1]), (0, 0))).astype(BF16)
        r, vv, kk, k_f, k_b, lw_f, lw_b, a_f, a_b, bonus, g = rwkv_prep(
            p_rw, mu, w0, w2, a0, a2, g2p, row(k_k[l]), row(k_a[l]), row(r_k[l]), seq, tm, c)
        y_f, y_bk = rwkv_scan(r, vv, kk, lw_f, k_f, a_f, lw_b, k_b, a_b, batch)
        y_b = rwkv_post(y_f, y_bk, bonus, g, ln_x_w[l], ln_x_b[l], tm)

        wr_hi, wr_lo = _hi_lo(w_router[l].T)
        h2, hn2, aff_t = merge_out_router(
            y_a, y_b, gates, h, w_branch_a[l].astype(BF16), w_branch_b[l].astype(BF16), w_out[l].astype(BF16),
            row(ffn_norm_g[l]), wr_hi, wr_lo, tm)

        pos, cnt = select_slots(aff_t, batch, cap)
        pos3 = pos.reshape(ne, 1, m)
        cnt = cnt.reshape(-1)
        xe, gate = moe_gather(cnt, pos3, aff_t.reshape(ne, 1, m), hn2, batch, cap)
        hid = moe_experts(xe, gate, w_gate_e[l], w_up_e[l])
        ye = moe_down(hid, w_down_e[l])
        h = moe_scatter(cnt, pos3, ye, h2, batch, min(1024, seq))
    return h.reshape(batch, seq, d)
```

```python
import functools
import math

import jax
import jax.numpy as jnp
from jax import lax
from jax.experimental import pallas as pl
from jax.experimental.pallas import tpu as pltpu

F32 = jnp.float32
BF16 = jnp.bfloat16

LANES = 128
HEAD = 64
CHUNK = 64
SCAN_CHUNKS = 4
RMS_EPS = 1e-6
GN_EPS = 64e-5
ROPE_THETA = 500000.0
ROPE_DIM = 16
VMEM_LIMIT = 56 * 1024 * 1024


def _cparams(*sem):
    return pltpu.CompilerParams(dimension_semantics=sem, vmem_limit_bytes=VMEM_LIMIT)


def _dot(a, b):
    return jnp.dot(a, b, preferred_element_type=F32)


def _dot_nt(a, b):
    return lax.dot_general(a, b, (((1,), (1,)), ((), ())), preferred_element_type=F32)


def _dot_tn(a, b):
    return lax.dot_general(a, b, (((0,), (0,)), ((), ())), preferred_element_type=F32)


def _split3(x):
    h1 = x.astype(BF16)
    r1 = x - h1.astype(F32)
    h2 = r1.astype(BF16)
    h3 = (r1 - h2.astype(F32)).astype(BF16)
    return h1, h2, h3


def _dot_exact_lhs(w_bf16, x):
    h1, h2, h3 = _split3(x)
    return _dot(w_bf16, h1) + _dot(w_bf16, h2) + _dot(w_bf16, h3)


def _scan_chains(r_ref, v_ref, kk_ref, lw_ref, k_ref, a_ref, y_ref, d, reverse, rows):
    L = CHUNK
    n_pairs = r_ref.shape[-1] // LANES
    t_i = lax.broadcasted_iota(jnp.int32, (L, L), 0)
    i_i = lax.broadcasted_iota(jnp.int32, (L, L), 1)
    tri = jnp.where((i_i >= t_i) if reverse else (i_i <= t_i), 1.0, 0.0).astype(BF16)

    t2 = lax.broadcasted_iota(jnp.int32, (L, 2 * LANES), 0)
    i2 = lax.broadcasted_iota(jnp.int32, (L, 2 * LANES), 1) % HEAD
    strict = (i2 > t2) if reverse else (i2 < t2)
    incl = (i2 >= t2) if reverse else (i2 <= t2)
    cum_all = _dot_exact_lhs(tri, lw_ref[rows, :])
    yield None
    last = 0 if reverse else L - 1
    for p in range(n_pairs):
        sl = slice(p * LANES, (p + 1) * LANES)
        cum = cum_all[:, sl]
        lw = lw_ref[rows, sl]
        tot = cum[last:last + 1, :]
        winv = jnp.exp(-cum)
        wrem = jnp.exp(tot - cum)
        kk = kk_ref[rows, sl]
        b = kk * a_ref[rows, sl]
        k = k_ref[rows, sl]
        rt = (r_ref[rows, sl] * jnp.exp(cum)).astype(BF16)
        at = (-kk * jnp.exp(cum - lw)).astype(BF16)
        yield dict(
            d=d, p=p, sl=sl, rows=rows, y_ref=y_ref, strict=strict, incl=incl,
            ar=jnp.concatenate([at, rt], axis=0), bt=(b * winv).astype(BF16), kt=(k * winv).astype(BF16),
            v=v_ref[rows, sl].astype(BF16), bh=(b * wrem).astype(BF16), kh=(k * wrem).astype(BF16),
            w_tot=jnp.exp(tot))


def _rwkv_scan_kernel(r_f, v_f, kk_f, lw_f, k_f, a_f, r_b, v_b, kk_b, lw_b, k_b, a_b, yf_ref, yb_ref, s_ref):
    @pl.when(pl.program_id(1) == 0)
    def _():
        s_ref[...] = jnp.zeros_like(s_ref)

    L = CHUNK
    n_sub = r_f.shape[0] // L
    def prep(i):
        fwd_rows = slice(i * L, (i + 1) * L)
        bwd_rows = slice((n_sub - 1 - i) * L, (n_sub - i) * L)
        yield from _scan_chains(r_f, v_f, kk_f, lw_f, k_f, a_f, yf_ref, 0, False, fwd_rows)
        yield from _scan_chains(r_b, v_b, kk_b, lw_b, k_b, a_b, yb_ref, 1, True, bwd_rows)

    cur = [c for c in prep(0) if c is not None]
    s = [s_ref[c["d"], c["p"]] for c in cur]
    for i in range(n_sub):
        pieces = prep(i + 1) if i + 1 < n_sub else iter(())
        nxt = []

        def fill(n_pieces):
            for _ in range(n_pieces):
                c = next(pieces, None)
                if c is not None:
                    nxt.append(c)

        s = _scan_stages(cur, s, fill)
        for c in pieces:
            if c is not None:
                nxt.append(c)
        cur = nxt or cur
    for c, si in zip(cur, s):
        s_ref[c["d"], c["p"]] = si


def _scan_stages(cs, s, fill):
    L = CHUNK
    lo = lax.broadcasted_iota(jnp.int32, (1, LANES), 1) < HEAD
    rr = lax.broadcasted_iota(jnp.int32, (LANES, LANES), 0) < HEAD
    cc = lax.broadcasted_iota(jnp.int32, (LANES, LANES), 1) < HEAD
    same_head = rr == cc
    cat = lambda *xs: jnp.concatenate(xs, axis=0)
    zero = jnp.zeros((), BF16)

    def bd(t):
        t = t.astype(BF16)
        return cat(jnp.where(lo, t, zero), jnp.where(lo, zero, t))

    g = [_dot_nt(c["ar"], cat(bd(c["bt"]), bd(c["kt"]))) for c in cs]
    m = [jnp.where(c["strict"], gi[:L], 0.0).astype(BF16) for c, gi in zip(cs, g)]
    n = [jnp.where(c["incl"], gi[L:], 0.0).astype(BF16) for c, gi in zip(cs, g)]
    fill(2)
    ars = [_dot_nt(c["ar"], si.astype(BF16)) for c, si in zip(cs, s)]
    vbd = [bd(c["v"]) for c in cs]
    fill(2)
    x = [a[:L] + _dot(mi[:, LANES:], vb) for a, mi, vb in zip(ars, m, vbd)]
    pw = [mi[:, :LANES] for mi in m]
    fill(2)
    n_sq = int(math.log2(L))
    for j in range(n_sq):
        if j + 1 < n_sq:
            px = [_dot(pi, jnp.concatenate([bd(xi), bd(pi)], axis=1)) for xi, pi in zip(x, pw)]
            x = [xi + pxi[:, :LANES] for xi, pxi in zip(x, px)]
            pw = [pxi[:, LANES:].astype(BF16) for pxi in px]
        else:
            x = [xi + _dot(pi, bd(xi)) for xi, pi in zip(x, pw)]
        fill(2)
    y = [a[L:] + _dot(ni, cat(bd(xi), vb)) for a, ni, xi, vb in zip(ars, n, x, vbd)]
    for c, yi in zip(cs, y):
        c["y_ref"][c["rows"], c["sl"]] = yi
    ds = [_dot_tn(cat(xi.astype(BF16), c["v"]), cat(c["bh"], c["kh"])) for c, xi in zip(cs, x)]
    return [si * c["w_tot"] + jnp.where(same_head, di, 0.0) for c, si, di in zip(cs, s, ds)]


def rwkv_scan(r, v, kk, lw_f, k_f, a_f, lw_b, k_b, a_b, batch):
    m, c = r.shape
    rows = SCAN_CHUNKS * CHUNK
    nc = m // batch // rows
    fwd = pl.BlockSpec((rows, c), lambda bi, ci: (bi * nc + ci, 0))
    bwd = pl.BlockSpec((rows, c), lambda bi, ci: (bi * nc + nc - 1 - ci, 0))
    return pl.pallas_call(
        _rwkv_scan_kernel,
        grid=(batch, nc),
        in_specs=[fwd] * 6 + [bwd] * 6,
        out_specs=[fwd, bwd],
        out_shape=[jax.ShapeDtypeStruct((m, c), F32)] * 2,
        scratch_shapes=[pltpu.VMEM((2, c // LANES, LANES, LANES), F32)],
        compiler_params=_cparams("arbitrary", "arbitrary"),
    )(r, v, kk, lw_f, k_f, a_f, r, v, kk, lw_b, k_b, a_b)


def _sigmoid(x):
    return 1.0 / (1.0 + jnp.exp(-x))


def _split2(x):
    hi = x.astype(BF16)
    return hi, (x - hi.astype(F32)).astype(BF16)


def _dot_f32(a, w_hi, w_lo):
    a_hi, a_lo = _split2(a)
    return _dot(a_hi, w_hi) + _dot(a_lo, w_hi) + _dot(a_hi, w_lo)


def _head_block_ones():
    r = lax.broadcasted_iota(jnp.int32, (LANES, LANES), 0) // HEAD
    c = lax.broadcasted_iota(jnp.int32, (LANES, LANES), 1) // HEAD
    return jnp.where(r == c, 1.0, 0.0).astype(BF16)


def _group_sum(x, ones):
    def tile_sum(t):
        hi, lo = _split2(t)
        return _dot(hi, ones) + _dot(lo, ones)

    tiles = [tile_sum(x[:, j * LANES:(j + 1) * LANES]) for j in range(x.shape[1] // LANES)]
    return tiles[0] if len(tiles) == 1 else jnp.concatenate(tiles, axis=1)


def _rmsnorm_kernel(x_ref, g_ref, o_ref):
    x = x_ref[...]
    y = x * lax.rsqrt(jnp.mean(x * x, axis=-1, keepdims=True) + RMS_EPS)
    o_ref[...] = (y * g_ref[...]).astype(o_ref.dtype)


def rmsnorm_bf16(x, g, tm):
    m, d = x.shape
    return pl.pallas_call(
        _rmsnorm_kernel,
        grid=(m // tm,),
        in_specs=[pl.BlockSpec((tm, d), lambda i: (i, 0)), pl.BlockSpec((1, d), lambda i: (0, 0))],
        out_specs=pl.BlockSpec((tm, d), lambda i: (i, 0)),
        out_shape=jax.ShapeDtypeStruct((m, d), BF16),
        compiler_params=_cparams("parallel"),
    )(x, g.reshape(1, d))


def _split_kernel(w_ref, qk_ref, v_ref, rw_ref, gt_ref, *, widths):
    outs = (qk_ref, v_ref, rw_ref, gt_ref)
    start = 0
    for o_ref, width in zip(outs, widths):
        part = w_ref[:, start:start + width].astype(o_ref.dtype)
        if o_ref.shape[1] > width:
            part = jnp.concatenate([part, jnp.zeros((part.shape[0], o_ref.shape[1] - width), o_ref.dtype)], axis=1)
        o_ref[...] = part
        start += width


def split_in_proj(w_in, layer, widths, rw_pad):
    depth, k, n = w_in.shape
    tk = 256 if k % 256 == 0 else k
    out_w = (widths[0], widths[1], rw_pad, widths[3])
    return pl.pallas_call(
        functools.partial(_split_kernel, widths=widths),
        grid=(k // tk,),
        in_specs=[pl.BlockSpec((tk, n), lambda i: (layer * (k // tk) + i, 0))],
        out_specs=[pl.BlockSpec((tk, w), lambda i: (i, 0)) for w in out_w],
        out_shape=[jax.ShapeDtypeStruct((k, w), BF16) for w in out_w],
        compiler_params=_cparams("parallel"),
    )(w_in.reshape(depth * k, n))


def _matmul_kernel(a_ref, w_ref, o_ref, *, act):
    acc = _dot(a_ref[...], w_ref[...])
    if act == "sigmoid":
        acc = _sigmoid(acc)
    o_ref[...] = acc.astype(o_ref.dtype)


def matmul(a, w, out_dtype, tm, tn, act=None):
    m, k = a.shape
    n = w.shape[1]
    return pl.pallas_call(
        functools.partial(_matmul_kernel, act=act),
        grid=(n // tn, m // tm),
        in_specs=[pl.BlockSpec((tm, k), lambda j, i: (i, 0)), pl.BlockSpec((k, tn), lambda j, i: (0, j))],
        out_specs=pl.BlockSpec((tm, tn), lambda j, i: (i, j)),
        out_shape=jax.ShapeDtypeStruct((m, n), out_dtype),
        compiler_params=_cparams("parallel", "parallel"),
    )(a, w)


def _qk_prep_kernel(qk_ref, cos_ref, sin_a_ref, sin_b_ref, gq_ref, gk_ref, q_ref, k_ref):
    ones = _head_block_ones()
    cos, sin_a, sin_b = cos_ref[...], sin_a_ref[...], sin_b_ref[...]
    n_tiles = q_ref.shape[1] // LANES
    for j in range(2 * n_tiles):
        x = qk_ref[:, j * LANES:(j + 1) * LANES]
        ms = _group_sum(x * x, ones) * (1.0 / HEAD)
        is_q = j < n_tiles
        y = x * lax.rsqrt(ms + RMS_EPS) * (gq_ref[...] if is_q else gk_ref[...])
        y = y * cos + pltpu.roll(y, LANES - ROPE_DIM // 2, 1) * sin_a + pltpu.roll(y, ROPE_DIM // 2, 1) * sin_b
        if is_q:
            q_ref[:, j * LANES:(j + 1) * LANES] = (y * HEAD ** -0.5).astype(q_ref.dtype)
        else:
            jj = j - n_tiles
            k_ref[:, jj * LANES:(jj + 1) * LANES] = y.astype(k_ref.dtype)


def qk_prep(qk, cos, sin_a, sin_b, gq, gk, seq, tm):
    m, w2 = qk.shape
    w = w2 // 2
    st = seq // tm
    tab = pl.BlockSpec((tm, LANES), lambda i: (i % st, 0))
    vec = pl.BlockSpec((1, LANES), lambda i: (0, 0))
    return pl.pallas_call(
        _qk_prep_kernel,
        grid=(m // tm,),
        in_specs=[pl.BlockSpec((tm, w2), lambda i: (i, 0)), tab, tab, tab, vec, vec],
        out_specs=[pl.BlockSpec((tm, w), lambda i: (i, 0))] * 2,
        out_shape=[jax.ShapeDtypeStruct((m, w), BF16)] * 2,
        compiler_params=_cparams("parallel"),
    )(qk, cos, sin_a, sin_b, gq, gk)


KEY_CHUNK = 512
MAX_SCORE_BOUND = 40.0


def _attn_kernel(bound_ref, lam_ref, q_ref, k_ref, v_ref, g_ref, o_ref, vaug_ref, *, lam_init):
    lp = lam_ref[...]
    lam = (jnp.exp(jnp.sum(lp[0:1] * lp[1:2], axis=-1, keepdims=True))
           - jnp.exp(jnp.sum(lp[2:3] * lp[3:4], axis=-1, keepdims=True)) + lam_init)
    seq = k_ref.shape[0]
    tq = q_ref.shape[0]
    tk = min(KEY_CHUNK, seq)

    @pl.when(pl.program_id(2) == 0)
    def _():
        vaug_ref[:, :LANES] = v_ref[...]
        vaug_ref[:, LANES:] = jnp.ones((seq, LANES), BF16)

    q = q_ref[...]
    lo = lax.broadcasted_iota(jnp.int32, (1, LANES), 1) < HEAD
    zero = jnp.zeros_like(q)
    q_lo, q_hi = jnp.where(lo, q, zero), jnp.where(lo, zero, q)

    def finish(o2):
        o = o2[:tq] - lam * o2[tq:]
        o = o * lax.rsqrt(jnp.mean(o * o, axis=-1, keepdims=True) + RMS_EPS)
        o_ref[...] = (o * g_ref[...] * (1.0 - lam_init)).astype(o_ref.dtype)

    bound = bound_ref[0, 0]
    safe = bound <= MAX_SCORE_BOUND

    @pl.when(safe)
    def _():
        q2 = jnp.concatenate([q_lo, q_hi], axis=0)
        acc = jnp.zeros((2 * tq, 2 * LANES), F32)
        for c in range(seq // tk):
            s = _dot_nt(q2, k_ref[c * tk:(c + 1) * tk, :])
            acc = acc + _dot(jnp.exp(s - bound).astype(BF16), vaug_ref[c * tk:(c + 1) * tk, :])
        finish(acc[:, :LANES] / acc[:, LANES:LANES + 1])

    @pl.when(jnp.logical_not(safe))
    def _():
        def branch(qm):
            s = _dot_nt(qm, k_ref[...])
            p = jnp.exp(s - jnp.max(s, axis=-1, keepdims=True))
            return _dot(p.astype(BF16), v_ref[...]) / jnp.sum(p, axis=-1, keepdims=True)

        finish(jnp.concatenate([branch(q_lo), branch(q_hi)], axis=0))


def diff_attention(q, k, v, score_bound, lam_params, subln_g, batch, lam_init, tq):
    m, w = q.shape
    seq = m // batch
    nq = seq // tq
    return pl.pallas_call(
        functools.partial(_attn_kernel, lam_init=lam_init),
        grid=(batch, w // LANES, nq),
        in_specs=[pl.BlockSpec(memory_space=pltpu.SMEM),
                  pl.BlockSpec((4, HEAD), lambda b, h, i: (0, 0)),
                  pl.BlockSpec((tq, LANES), lambda b, h, i: (b * nq + i, h)),
                  pl.BlockSpec((seq, LANES), lambda b, h, i: (b, h)),
                  pl.BlockSpec((seq, LANES), lambda b, h, i: (b, h)),
                  pl.BlockSpec((1, LANES), lambda b, h, i: (0, 0))],
        out_specs=pl.BlockSpec((tq, LANES), lambda b, h, i: (b * nq + i, h)),
        out_shape=jax.ShapeDtypeStruct((m, w), BF16),
        scratch_shapes=[pltpu.VMEM((seq, 2 * LANES), BF16)],
        compiler_params=_cparams("parallel", "parallel", "arbitrary"),
    )(score_bound, lam_params, q, k, v, subln_g.reshape(1, LANES))


def _rwkv_prep_kernel(p_ref, prev_ref, next_ref, mu_ref, w0_ref, w2h_ref, w2l_ref, a0_ref, a2_ref,
                      g2_ref, kkw_ref, ka_ref, rk_ref,
                      r_o, v_o, kk_o, kf_o, kb_o, lwf_o, lwb_o, af_o, ab_o, bonus_o, g_o, *, seq_tiles):
    tm = p_ref.shape[0]
    c = r_o.shape[1]
    ti = pl.program_id(0) % seq_tiles
    p = p_ref[...]
    row = lax.broadcasted_iota(jnp.int32, (tm, 1), 0)
    prev_row = jnp.where(ti > 0, prev_ref[7:8, :], 0.0)
    next_row = jnp.where(ti < seq_tiles - 1, next_ref[0:1, :], 0.0)
    prev = jnp.where(row == 0, prev_row, pltpu.roll(p, 1, 0))
    nxt = jnp.where(row == tm - 1, next_row, pltpu.roll(p, tm - 1, 0))
    p = p + mu_ref[...] * (0.5 * (prev + nxt) - p)

    r, k, v = p[:, :c], p[:, c:2 * c], p[:, 2 * c:3 * c]
    wd = p[:, 3 * c:3 * c + LANES]
    ad = p[:, 3 * c + LANES:3 * c + 2 * LANES]
    gd = p[:, 3 * c + 2 * LANES:]
    wl = w0_ref[...] + _dot_f32(jnp.tanh(wd), w2h_ref[...], w2l_ref[...])
    lw = -math.exp(-0.5) * _sigmoid(wl)
    a = _sigmoid(a0_ref[...] + _dot(ad.astype(BF16), a2_ref[...]))
    g_o[...] = _dot(_sigmoid(gd).astype(BF16), g2_ref[...])
    a_f, a_b = a[:, :c], a[:, c:]
    ones = _head_block_ones()
    kk = k * kkw_ref[...]
    kk = kk / jnp.maximum(jnp.sqrt(_group_sum(kk * kk, ones)), 1e-12)
    k_a = ka_ref[...]
    k_f = k * (1.0 + (a_f - 1.0) * k_a)
    k_b = k * (1.0 + (a_b - 1.0) * k_a)
    bonus_o[...] = _group_sum(r * (k_f + k_b) * rk_ref[...], ones) * v
    r_o[...] = r
    v_o[...] = v
    kk_o[...] = kk
    kf_o[...] = k_f
    kb_o[...] = k_b
    lwf_o[...] = lw[:, :c]
    lwb_o[...] = lw[:, c:]
    af_o[...] = a_f
    ab_o[...] = a_b


def rwkv_prep(p, mu, w0, w2, a0, a2, g2, k_k, k_a, r_k, seq, tm, c):
    m, pc = p.shape
    st = seq // tm
    hb = tm // 8
    last = m // 8 - 1
    full = lambda a: pl.BlockSpec(a.shape, lambda i: (0, 0))
    row_c = pl.BlockSpec((tm, c), lambda i: (i, 0))
    small = [mu, w0, w2[0], w2[1], a0, a2, g2, k_k, k_a, r_k]
    return pl.pallas_call(
        functools.partial(_rwkv_prep_kernel, seq_tiles=st),
        grid=(m // tm,),
        in_specs=[pl.BlockSpec((tm, pc), lambda i: (i, 0)),
                  pl.BlockSpec((8, pc), lambda i: (jnp.maximum(i * hb - 1, 0), 0)),
                  pl.BlockSpec((8, pc), lambda i: (jnp.minimum((i + 1) * hb, last), 0))] + [full(a) for a in small],
        out_specs=[row_c] * 11,
        out_shape=[jax.ShapeDtypeStruct((m, c), F32)] * 11,
        compiler_params=_cparams("parallel"),
    )(p, p, p, *small)


def _rwkv_post_kernel(yf_ref, yb_ref, bonus_ref, g_ref, lnw_ref, lnb_ref, o_ref):
    ones = _head_block_ones()
    y = yf_ref[...] + yb_ref[...]
    mean = _group_sum(y, ones) * (1.0 / HEAD)
    yc = y - mean
    var = _group_sum(yc * yc, ones) * (1.0 / HEAD)
    yn = yc * lax.rsqrt(var + GN_EPS) * lnw_ref[...] + lnb_ref[...]
    o_ref[...] = ((yn + bonus_ref[...]) * g_ref[...]).astype(o_ref.dtype)


def rwkv_post(yf, yb, bonus, g, ln_w, ln_b, tm):
    m, c = yf.shape
    row = pl.BlockSpec((tm, c), lambda i: (i, 0))
    vec = pl.BlockSpec((1, c), lambda i: (0, 0))
    return pl.pallas_call(
        _rwkv_post_kernel,
        grid=(m // tm,),
        in_specs=[row] * 4 + [vec] * 2,
        out_specs=row,
        out_shape=jax.ShapeDtypeStruct((m, c), BF16),
        compiler_params=_cparams("parallel"),
    )(yf, yb, bonus, g, ln_w.reshape(1, c), ln_b.reshape(1, c))


MERGE_ROWS = 128


def _merge_kernel(ya_ref, yb_ref, gate_ref, x_ref, wa_ref, wb_ref, wo_ref, fg_ref, wrh_ref, wrl_ref,
                  h_ref, hn_ref, aff_ref):
    d = x_ref.shape[1]
    tm = x_ref.shape[0]
    sub = min(MERGE_ROWS, tm)
    rows = [slice(r * sub, (r + 1) * sub) for r in range(tm // sub)]
    pa = [_dot(ya_ref[r, :], wa_ref[...]) for r in rows]
    pb = [_dot(yb_ref[r, :], wb_ref[...]) for r in rows]
    merged = [(gate_ref[r, :d].astype(F32) * a + gate_ref[r, d:].astype(F32) * b).astype(BF16)
              for r, a, b in zip(rows, pa, pb)]
    hs = [x_ref[r, :] + _dot(mg, wo_ref[...]) for r, mg in zip(rows, merged)]
    for r, h in zip(rows, hs):
        h_ref[r, :] = h
        hn = h * lax.rsqrt(jnp.mean(h * h, axis=-1, keepdims=True) + RMS_EPS) * fg_ref[...]
        hn_ref[r, :] = hn.astype(hn_ref.dtype)
        hn_hi, hn_lo = _split2(hn)
        wr_hi = wrh_ref[...]
        logits = _dot_nt(wr_hi, hn_hi) + _dot_nt(wr_hi, hn_lo) + _dot_nt(wrl_ref[...], hn_hi)
        e = jnp.exp(logits - jnp.max(logits, axis=0, keepdims=True))
        aff_ref[:, r] = e / jnp.sum(e, axis=0, keepdims=True)


def merge_out_router(ya, yb, gates, x, wa, wb, wo, ffn_g, wr_hi, wr_lo, tm):
    m, d = x.shape
    full = lambda a: pl.BlockSpec(a.shape, lambda i: (0, 0), pipeline_mode=pl.Buffered(1))
    row = lambda a: pl.BlockSpec((tm, a.shape[1]), lambda i: (i, 0))
    ne = wr_hi.shape[0]
    return pl.pallas_call(
        _merge_kernel,
        grid=(m // tm,),
        in_specs=[row(ya), row(yb), row(gates), row(x), full(wa), full(wb), full(wo), full(ffn_g),
                  full(wr_hi), full(wr_lo)],
        out_specs=[pl.BlockSpec((tm, d), lambda i: (i, 0)), pl.BlockSpec((tm, d), lambda i: (i, 0)),
                   pl.BlockSpec((ne, tm), lambda i: (0, i))],
        out_shape=[jax.ShapeDtypeStruct((m, d), F32), jax.ShapeDtypeStruct((m, d), BF16),
                   jax.ShapeDtypeStruct((ne, m), F32)],
        compiler_params=_cparams("parallel"),
    )(ya, yb, gates, x, wa, wb, wo, ffn_g, wr_hi, wr_lo)


def _prefix_excl(m_bf16, upper):
    rows, t = m_bf16.shape
    off = jnp.zeros((rows, 1), F32)
    out = []
    for j in range(t // LANES):
        blk = m_bf16[:, j * LANES:(j + 1) * LANES]
        out.append(_dot(blk, upper) + off)
        off = off + jnp.sum(blk.astype(F32), axis=-1, keepdims=True)
    return jnp.concatenate(out, axis=1)


TOKEN_BLOCK = 256
SLOT_WINDOW = 128
SLOT_ALIGN = 16


def _select_kernel(aff_ref, pos_ref, cnt_ref, *, cap):
    aff = aff_ref[...]
    count = lambda pred: jnp.sum(jnp.where(pred, 1.0, 0.0), axis=-1, keepdims=True)
    as_f32 = lambda bits: pltpu.bitcast(jnp.broadcast_to(bits, aff.shape), F32)

    def step(i, thr):
        cand = thr | (jnp.int32(1) << (30 - i))
        return jnp.where(count(aff >= as_f32(cand)) >= cap, cand, thr)

    thr = lax.fori_loop(0, 31, step, jnp.zeros((aff.shape[0], 1), jnp.int32))
    gt = aff >= as_f32(thr + 1)
    eq = (aff >= as_f32(thr)) & jnp.logical_not(gt)
    r = lax.broadcasted_iota(jnp.int32, (LANES, LANES), 0)
    c = lax.broadcasted_iota(jnp.int32, (LANES, LANES), 1)
    upper = jnp.where(r < c, 1.0, 0.0).astype(BF16)
    need = cap - jnp.sum(jnp.where(gt, 1.0, 0.0), axis=-1, keepdims=True)
    eq_rank = _prefix_excl(jnp.where(eq, 1.0, 0.0).astype(BF16), upper)
    sel = gt | (eq & (eq_rank < need))
    sel_bf = jnp.where(sel, 1.0, 0.0).astype(BF16)
    pos = _prefix_excl(sel_bf, upper)
    pos_ref[...] = jnp.where(sel, pos, -1.0).astype(jnp.int32)
    seq = aff.shape[1]
    tok = lax.broadcasted_iota(jnp.int32, (seq, LANES), 0)
    blk = lax.broadcasted_iota(jnp.int32, (seq, LANES), 1)
    before = jnp.where(tok < blk * TOKEN_BLOCK, 1.0, 0.0).astype(BF16)
    cnt_ref[0] = _dot(sel_bf, before).astype(jnp.int32)


def select_slots(aff_t, batch, cap):
    ne, m = aff_t.shape
    seq = m // batch
    assert seq // TOKEN_BLOCK < LANES
    return pl.pallas_call(
        functools.partial(_select_kernel, cap=cap),
        grid=(batch,),
        in_specs=[pl.BlockSpec((ne, seq), lambda b: (0, b))],
        out_specs=[pl.BlockSpec((ne, seq), lambda b: (0, b)), pl.BlockSpec((1, ne, LANES), lambda b: (b, 0, 0))],
        out_shape=[jax.ShapeDtypeStruct((ne, m), jnp.int32), jax.ShapeDtypeStruct((batch, ne, LANES), jnp.int32)],
        compiler_params=_cparams("parallel"),
    )(aff_t)


def _for_each_window(cnt_ref, bases, n_blocks, sw, cap, step):
    def start(x, j):
        return (cnt_ref[bases[x] + j] // SLOT_ALIGN) * SLOT_ALIGN

    def window(x, lo):
        s0 = pl.multiple_of(jnp.minimum(lo, cap - sw), SLOT_ALIGN)
        slot = s0 + lax.broadcasted_iota(jnp.int32, (sw, 1), 0)
        return x, s0, lambda pos_blk: (pos_blk == slot) & (slot >= lo)

    for j in range(n_blocks):
        step(j, [window(x, start(x, j)) for x in range(len(bases))])
    for j in range(n_blocks):
        for x in range(len(bases)):
            lo0 = start(x, j)
            n_win = (cnt_ref[bases[x] + j + 1] - lo0 + sw - 1) // sw

            def extra(k, carry, j=j, x=x, lo0=lo0):
                step(j, [window(x, lo0 + k * sw)])
                return carry

            lax.fori_loop(1, n_win, extra, 0)


GATHER_GROUP = 1
SCATTER_GROUP = 2


def _group_bases(batch_id, group_id, n_groups, group):
    first = (batch_id * n_groups + group_id) * group
    return [(first + x) * LANES for x in range(group)]


def _gather_kernel(cnt_ref, pos_ref, aff_ref, hn_ref, xe_ref, gate_ref, acc_ref, gacc_ref):
    seq = hn_ref.shape[0]
    group, cap, _ = acc_ref.shape
    tb = min(TOKEN_BLOCK, seq)
    sw = min(SLOT_WINDOW, cap)
    bases = _group_bases(pl.program_id(0), pl.program_id(1), pl.num_programs(1), group)
    acc_ref[...] = jnp.zeros_like(acc_ref)
    gacc_ref[...] = jnp.zeros_like(gacc_ref)

    def step(j, windows):
        tok = slice(j * tb, (j + 1) * tb)
        hits = [hit_of(pos_ref[x, :, tok]) for x, _, hit_of in windows]
        onehot = jnp.concatenate([jnp.where(h, 1.0, 0.0).astype(BF16) for h in hits], axis=0)
        rows = _dot(onehot, hn_ref[tok, :])
        for i, ((x, s0, _), h) in enumerate(zip(windows, hits)):
            acc_ref[x, pl.ds(s0, sw), :] += rows[i * sw:(i + 1) * sw]
            gacc_ref[x, pl.ds(s0, sw), :] += jnp.sum(jnp.where(h, aff_ref[x, :, tok], 0.0), axis=-1, keepdims=True)

    _for_each_window(cnt_ref, bases, seq // tb, sw, cap, step)
    xe_ref[:, 0] = acc_ref[...].astype(xe_ref.dtype)
    gate_ref[:, 0] = gacc_ref[...]


def moe_gather(cnt, pos3, aff3, hn, batch, cap):
    ne = pos3.shape[0]
    m, d = hn.shape
    seq = m // batch
    g = GATHER_GROUP
    assert cap % min(SLOT_WINDOW, cap) == 0 and ne % g == 0
    row = pl.BlockSpec((g, 1, seq), lambda b, e, c: (e, 0, b))
    return pl.pallas_call(
        _gather_kernel,
        grid_spec=pltpu.PrefetchScalarGridSpec(
            num_scalar_prefetch=1,
            grid=(batch, ne // g),
            in_specs=[row, row, pl.BlockSpec((seq, d), lambda b, e, c: (b, 0))],
            out_specs=[pl.BlockSpec((g, 1, cap, d), lambda b, e, c: (e, b, 0, 0)),
                       pl.BlockSpec((g, 1, cap, 1), lambda b, e, c: (e, b, 0, 0))],
            scratch_shapes=[pltpu.VMEM((g, cap, d), F32), pltpu.VMEM((g, cap, 1), F32)]),
        out_shape=[jax.ShapeDtypeStruct((ne, batch, cap, d), BF16),
                   jax.ShapeDtypeStruct((ne, batch, cap, 1), F32)],
        compiler_params=_cparams("parallel", "parallel"),
    )(cnt, pos3, aff3, hn)


def _expert_kernel(xe_ref, gate_ref, wg_ref, wu_ref, hid_ref, wg_bf, wu_bf):
    @pl.when(pl.program_id(1) == 0)
    def _():
        wg_bf[...] = wg_ref[0].astype(BF16)
        wu_bf[...] = wu_ref[0].astype(BF16)

    xe = xe_ref[0, 0]
    hg = _dot(xe, wg_bf[...])
    hu = _dot(xe, wu_bf[...])
    hid_ref[0, 0] = (hg * _sigmoid(hg) * hu * gate_ref[0, 0]).astype(hid_ref.dtype)


def moe_experts(xe, gate, wg, wu):
    ne, batch, cap, d = xe.shape
    ff = wg.shape[2]
    return pl.pallas_call(
        _expert_kernel,
        grid=(ne, batch),
        in_specs=[pl.BlockSpec((1, 1, cap, d), lambda e, b: (e, b, 0, 0)),
                  pl.BlockSpec((1, 1, cap, 1), lambda e, b: (e, b, 0, 0)),
                  pl.BlockSpec((1, d, ff), lambda e, b: (e, 0, 0)),
                  pl.BlockSpec((1, d, ff), lambda e, b: (e, 0, 0))],
        out_specs=pl.BlockSpec((1, 1, cap, ff), lambda e, b: (e, b, 0, 0)),
        out_shape=jax.ShapeDtypeStruct((ne, batch, cap, ff), BF16),
        scratch_shapes=[pltpu.VMEM((d, ff), BF16)] * 2,
        compiler_params=_cparams("parallel", "arbitrary"),
    )(xe, gate, wg, wu)


def _down_kernel(hid_ref, wd_ref, ye_ref, wd_bf):
    @pl.when(pl.program_id(1) == 0)
    def _():
        wd_bf[...] = wd_ref[0].astype(BF16)

    ye_ref[0, 0] = _dot(hid_ref[0, 0], wd_bf[...]).astype(ye_ref.dtype)


def moe_down(hid, wd):
    ne, batch, cap, ff = hid.shape
    d = wd.shape[2]
    return pl.pallas_call(
        _down_kernel,
        grid=(ne, batch),
        in_specs=[pl.BlockSpec((1, 1, cap, ff), lambda e, b: (e, b, 0, 0)),
                  pl.BlockSpec((1, ff, d), lambda e, b: (e, 0, 0))],
        out_specs=pl.BlockSpec((1, 1, cap, d), lambda e, b: (e, b, 0, 0)),
        out_shape=jax.ShapeDtypeStruct((ne, batch, cap, d), BF16),
        scratch_shapes=[pltpu.VMEM((ff, d), BF16)],
        compiler_params=_cparams("parallel", "arbitrary"),
    )(hid, wd)


def _scatter_kernel(cnt_ref, pos_ref, ye_ref, h_ref, o_ref):
    e = pl.program_id(2)
    tt = h_ref.shape[0]
    group, _, cap, _ = ye_ref.shape
    tb = min(TOKEN_BLOCK, tt)
    sw = min(SLOT_WINDOW, cap)
    tile_off = pl.program_id(1) * (tt // tb)
    bases = [b + tile_off for b in _group_bases(pl.program_id(0), e, pl.num_programs(2), group)]

    @pl.when(e == 0)
    def _():
        o_ref[...] = h_ref[...]

    def step(j, windows):
        tok = slice(j * tb, (j + 1) * tb)
        onehot = jnp.concatenate([jnp.where(hit_of(pos_ref[x, :, tok]), 1.0, 0.0).astype(BF16)
                                  for x, _, hit_of in windows], axis=0)
        ye = jnp.concatenate([ye_ref[x, 0, pl.ds(s0, sw), :] for x, s0, _ in windows], axis=0)
        o_ref[tok, :] += _dot_tn(onehot, ye)

    _for_each_window(cnt_ref, bases, tt // tb, sw, cap, step)


def moe_scatter(cnt, pos3, ye, h, batch, tt):
    ne, _, cap, d = ye.shape
    m = h.shape[0]
    nt = m // batch // tt
    g = SCATTER_GROUP
    assert ne % g == 0
    return pl.pallas_call(
        _scatter_kernel,
        grid_spec=pltpu.PrefetchScalarGridSpec(
            num_scalar_prefetch=1,
            grid=(batch, nt, ne // g),
            in_specs=[pl.BlockSpec((g, 1, tt), lambda b, i, e, c: (e, 0, b * nt + i)),
                      pl.BlockSpec((g, 1, cap, d), lambda b, i, e, c: (e, b, 0, 0)),
                      pl.BlockSpec((tt, d), lambda b, i, e, c: (b * nt + i, 0))],
            out_specs=pl.BlockSpec((tt, d), lambda b, i, e, c: (b * nt + i, 0))),
        out_shape=jax.ShapeDtypeStruct((m, d), F32),
        compiler_params=_cparams("parallel", "parallel", "arbitrary"),
    )(cnt, pos3, ye, h)


def _rope_tables(seq):
    half = ROPE_DIM // 2
    inv = ROPE_THETA ** (-(jnp.arange(0, ROPE_DIM, 2, dtype=F32) / ROPE_DIM))
    ang = jnp.arange(seq, dtype=F32)[:, None] * inv[None, :]
    cos, sin = jnp.cos(ang), jnp.sin(ang)
    one = jnp.ones((seq, HEAD - ROPE_DIM), F32)
    zero = lambda n: jnp.zeros((seq, n), F32)
    cos_t = jnp.concatenate([cos, cos, one], axis=1)
    sin_a = jnp.concatenate([-sin, zero(HEAD - half)], axis=1)
    sin_b = jnp.concatenate([zero(half), sin, zero(HEAD - ROPE_DIM)], axis=1)
    return [jnp.tile(t, (1, LANES // HEAD)) for t in (cos_t, sin_a, sin_b)]


def _hi_lo(w):
    hi = w.astype(BF16)
    return hi, (w - hi.astype(F32)).astype(BF16)


def _block_diag2(wf, wb):
    z = jnp.zeros_like(wf)
    return jnp.concatenate([jnp.concatenate([wf, z], axis=1), jnp.concatenate([z, wb], axis=1)], axis=0)


def _col_tile(n):
    for t in (1024, 896, 512, 256, 128):
        if n % t == 0:
            return t
    raise ValueError(f"unsupported matmul width {n}")


def kernel(x, attn_norm_g, w_in, q_norm_g, k_norm_g, lambda_q1, lambda_k1, lambda_q2, lambda_k2, subln_g, shift_mu, w0_f, w2_f, w0_b, w2_b, a0_f, a2_f, a0_b, a2_b, g2, k_k, k_a, r_k, ln_x_w, ln_x_b, w_branch_a, w_branch_b, w_out, ffn_norm_g, w_router, w_gate_e, w_up_e, w_down_e):
    batch, seq, d = x.shape
    m = batch * seq
    depth = w_in.shape[0]
    c = w_branch_b.shape[1]
    qk_w = 2 * w_branch_a.shape[1]
    v_w = w_branch_a.shape[1]
    rw_cols = shift_mu.shape[1]
    rw_pad = -(-rw_cols // LANES) * LANES
    ne = w_router.shape[2]
    cap = 2 * seq // ne
    tm = min(256, seq)
    tmm = next((t for t in (1024, 512) if m % t == 0), tm)
    cos_t, sin_a, sin_b = _rope_tables(seq)
    row = lambda a: a.reshape(1, -1)

    h = x.reshape(m, d)
    for l in range(depth):
        lam_init = 0.8 - 0.6 * math.exp(-0.3 * l)
        w_qk, w_v, w_rw, w_gt = split_in_proj(w_in, l, (qk_w, v_w, rw_cols, 2 * d), rw_pad)

        hn = rmsnorm_bf16(h, attn_norm_g[l], tmm)
        qk = matmul(hn, w_qk, F32, tmm, _col_tile(qk_w))
        v = matmul(hn, w_v, BF16, tmm, _col_tile(v_w))
        p_rw = matmul(hn, w_rw, F32, tmm, _col_tile(rw_pad))
        gates = matmul(hn, w_gt, BF16, tmm, _col_tile(2 * d), act="sigmoid")

        gq = row(jnp.tile(q_norm_g[l], LANES // HEAD))
        gk = row(jnp.tile(k_norm_g[l], LANES // HEAD))
        q, k = qk_prep(qk, cos_t, sin_a, sin_b, gq, gk, seq, tm)
        lam_params = jnp.stack([lambda_q1[l], lambda_k1[l], lambda_q2[l], lambda_k2[l]])
        score_bound = (1.02 * math.sqrt(HEAD) * jnp.max(jnp.abs(q_norm_g[l])) * jnp.max(jnp.abs(k_norm_g[l]))
                       ).astype(F32).reshape(1, 1)
        y_a = diff_attention(q, k, v, score_bound, lam_params, subln_g[l], batch, lam_init, min(512, seq))

        mu = jnp.pad(shift_mu[l], (0, rw_pad - rw_cols)).reshape(1, rw_pad)
        w0 = row(jnp.concatenate([w0_f[l], w0_b[l]]))
        a0 = row(jnp.concatenate([a0_f[l], a0_b[l]]))
        w2 = _hi_lo(_block_diag2(w2_f[l], w2_b[l]))
        a2 = _block_diag2(a2_f[l], a2_b[l]).astype(BF16)
        g_rows = rw_pad - 3 * c - 2 * LANES
        g2p = jnp.pad(g2[l], ((0, g_rows - g2.shape[1]), (0, 0))).astype(BF16)
        r, vv, kk, k_f, k_b, lw_f, lw_b, a_f, a_b, bonus, g = rwkv_prep(
            p_rw, mu, w0, w2, a0, a2, g2p, row(k_k[l]), row(k_a[l]), row(r_k[l]), seq, tm, c)
        y_f, y_bk = rwkv_scan(r, vv, kk, lw_f, k_f, a_f, lw_b, k_b, a_b, batch)
        y_b = rwkv_post(y_f, y_bk, bonus, g, ln_x_w[l], ln_x_b[l], tm)

        wr_hi, wr_lo = _hi_lo(w_router[l].T)
        h2, hn2, aff_t = merge_out_router(
            y_a, y_b, gates, h, w_branch_a[l].astype(BF16), w_branch_b[l].astype(BF16), w_out[l].astype(BF16),
            row(ffn_norm_g[l]), wr_hi, wr_lo, tm)

        pos, cnt = select_slots(aff_t, batch, cap)
        pos3 = pos.reshape(ne, 1, m)
        cnt = cnt.reshape(-1)
        xe, gate = moe_gather(cnt, pos3, aff_t.reshape(ne, 1, m), hn2, batch, cap)
        hid = moe_experts(xe, gate, w_gate_e[l], w_up_e[l])
        ye = moe_down(hid, w_down_e[l])
        h = moe_scatter(cnt, pos3, ye, h2, batch, min(1024, seq))
    return h.reshape(batch, seq, d)
```

```python
import functools
import math

import jax
import jax.numpy as jnp
from jax import lax
from jax.experimental import pallas as pl
from jax.experimental.pallas import tpu as pltpu

F32 = jnp.float32
BF16 = jnp.bfloat16

LANES = 128
HEAD = 64
CHUNK = 64
SCAN_CHUNKS = 4
RMS_EPS = 1e-6
GN_EPS = 64e-5
ROPE_THETA = 500000.0
ROPE_DIM = 16
VMEM_LIMIT = 56 * 1024 * 1024


def _cparams(*sem):
    return pltpu.CompilerParams(dimension_semantics=sem, vmem_limit_bytes=VMEM_LIMIT)


def _dot(a, b):
    return jnp.dot(a, b, preferred_element_type=F32)


def _dot_nt(a, b):
    return lax.dot_general(a, b, (((1,), (1,)), ((), ())), preferred_element_type=F32)


def _dot_tn(a, b):
    return lax.dot_general(a, b, (((0,), (0,)), ((), ())), preferred_element_type=F32)


def _split3(x):
    h1 = x.astype(BF16)
    r1 = x - h1.astype(F32)
    h2 = r1.astype(BF16)
    h3 = (r1 - h2.astype(F32)).astype(BF16)
    return h1, h2, h3


def _dot_exact_lhs(w_bf16, x):
    h1, h2, h3 = _split3(x)
    return _dot(w_bf16, h1) + _dot(w_bf16, h2) + _dot(w_bf16, h3)


def _scan_chains(r_ref, v_ref, kk_ref, lw_ref, k_ref, a_ref, y_ref, d, reverse, rows):
    L = CHUNK
    n_pairs = r_ref.shape[-1] // LANES
    t_i = lax.broadcasted_iota(jnp.int32, (L, L), 0)
    i_i = lax.broadcasted_iota(jnp.int32, (L, L), 1)
    tri = jnp.where((i_i >= t_i) if reverse else (i_i <= t_i), 1.0, 0.0).astype(BF16)

    t2 = lax.broadcasted_iota(jnp.int32, (L, 2 * LANES), 0)
    i2 = lax.broadcasted_iota(jnp.int32, (L, 2 * LANES), 1) % HEAD
    strict = (i2 > t2) if reverse else (i2 < t2)
    incl = (i2 >= t2) if reverse else (i2 <= t2)
    cum_all = _dot_exact_lhs(tri, lw_ref[rows, :])
    yield None
    last = 0 if reverse else L - 1
    for p in range(n_pairs):
        sl = slice(p * LANES, (p + 1) * LANES)
        cum = cum_all[:, sl]
        lw = lw_ref[rows, sl]
        tot = cum[last:last + 1, :]
        winv = jnp.exp(-cum)
        wrem = jnp.exp(tot - cum)
        kk = kk_ref[rows, sl]
        b = kk * a_ref[rows, sl]
        k = k_ref[rows, sl]
        rt = (r_ref[rows, sl] * jnp.exp(cum)).astype(BF16)
        at = (-kk * jnp.exp(cum - lw)).astype(BF16)
        yield dict(
            d=d, p=p, sl=sl, rows=rows, y_ref=y_ref, strict=strict, incl=incl,
            ar=jnp.concatenate([at, rt], axis=0), bt=(b * winv).astype(BF16), kt=(k * winv).astype(BF16),
            v=v_ref[rows, sl].astype(BF16), bh=(b * wrem).astype(BF16), kh=(k * wrem).astype(BF16),
            w_tot=jnp.exp(tot))


def _rwkv_scan_kernel(r_f, v_f, kk_f, lw_f, k_f, a_f, r_b, v_b, kk_b, lw_b, k_b, a_b, yf_ref, yb_ref, s_ref):
    @pl.when(pl.program_id(1) == 0)
    def _():
        s_ref[...] = jnp.zeros_like(s_ref)

    L = CHUNK
    n_sub = r_f.shape[0] // L
    def prep(i):
        fwd_rows = slice(i * L, (i + 1) * L)
        bwd_rows = slice((n_sub - 1 - i) * L, (n_sub - i) * L)
        yield from _scan_chains(r_f, v_f, kk_f, lw_f, k_f, a_f, yf_ref, 0, False, fwd_rows)
        yield from _scan_chains(r_b, v_b, kk_b, lw_b, k_b, a_b, yb_ref, 1, True, bwd_rows)

    cur = [c for c in prep(0) if c is not None]
    s = [s_ref[c["d"], c["p"]] for c in cur]
    for i in range(n_sub):
        pieces = prep(i + 1) if i + 1 < n_sub else iter(())
        nxt = []

        def fill(n_pieces):
            for _ in range(n_pieces):
                c = next(pieces, None)
                if c is not None:
                    nxt.append(c)

        s = _scan_stages(cur, s, fill)
        for c in pieces:
            if c is not None:
                nxt.append(c)
        cur = nxt or cur
    for c, si in zip(cur, s):
        s_ref[c["d"], c["p"]] = si


def _scan_stages(cs, s, fill):
    L = CHUNK
    lo = lax.broadcasted_iota(jnp.int32, (1, LANES), 1) < HEAD
    rr = lax.broadcasted_iota(jnp.int32, (LANES, LANES), 0) < HEAD
    cc = lax.broadcasted_iota(jnp.int32, (LANES, LANES), 1) < HEAD
    same_head = rr == cc
    cat = lambda *xs: jnp.concatenate(xs, axis=0)
    zero = jnp.zeros((), BF16)

    def bd(t):
        t = t.astype(BF16)
        return cat(jnp.where(lo, t, zero), jnp.where(lo, zero, t))

    g = [_dot_nt(c["ar"], cat(bd(c["bt"]), bd(c["kt"]))) for c in cs]
    m = [jnp.where(c["strict"], gi[:L], 0.0).astype(BF16) for c, gi in zip(cs, g)]
    n = [jnp.where(c["incl"], gi[L:], 0.0).astype(BF16) for c, gi in zip(cs, g)]
    fill(2)
    ars = [_dot_nt(c["ar"], si.astype(BF16)) for c, si in zip(cs, s)]
    vbd = [bd(c["v"]) for c in cs]
    fill(2)
    x = [a[:L] + _dot(mi[:, LANES:], vb) for a, mi, vb in zip(ars, m, vbd)]
    pw = [mi[:, :LANES] for mi in m]
    fill(2)
    n_sq = int(math.log2(L))
    for j in range(n_sq):
        if j + 1 < n_sq:
            px = [_dot(pi, jnp.concatenate([bd(xi), bd(pi)], axis=1)) for xi, pi in zip(x, pw)]
            x = [xi + pxi[:, :LANES] for xi, pxi in zip(x, px)]
            pw = [pxi[:, LANES:].astype(BF16) for pxi in px]
        else:
            x = [xi + _dot(pi, bd(xi)) for xi, pi in zip(x, pw)]
        fill(2)
    y = [a[L:] + _dot(ni, cat(bd(xi), vb)) for a, ni, xi, vb in zip(ars, n, x, vbd)]
    for c, yi in zip(cs, y):
        c["y_ref"][c["rows"], c["sl"]] = yi
    ds = [_dot_tn(cat(xi.astype(BF16), c["v"]), cat(c["bh"], c["kh"])) for c, xi in zip(cs, x)]
    return [si * c["w_tot"] + jnp.where(same_head, di, 0.0) for c, si, di in zip(cs, s, ds)]


def rwkv_scan(r, v, kk, lw_f, k_f, a_f, lw_b, k_b, a_b, batch):
    m, c = r.shape
    rows = SCAN_CHUNKS * CHUNK
    nc = m // batch // rows
    fwd = pl.BlockSpec((rows, c), lambda bi, ci: (bi * nc + ci, 0))
    bwd = pl.BlockSpec((rows, c), lambda bi, ci: (bi * nc + nc - 1 - ci, 0))
    return pl.pallas_call(
        _rwkv_scan_kernel,
        grid=(batch, nc),
        in_specs=[fwd] * 6 + [bwd] * 6,
        out_specs=[fwd, bwd],
        out_shape=[jax.ShapeDtypeStruct((m, c), F32)] * 2,
        scratch_shapes=[pltpu.VMEM((2, c // LANES, LANES, LANES), F32)],
        compiler_params=_cparams("arbitrary", "arbitrary"),
    )(r, v, kk, lw_f, k_f, a_f, r, v, kk, lw_b, k_b, a_b)


def _sigmoid(x):
    return 1.0 / (1.0 + jnp.exp(-x))


def _split2(x):
    hi = x.astype(BF16)
    return hi, (x - hi.astype(F32)).astype(BF16)


def _dot_f32(a, w_hi, w_lo):
    a_hi, a_lo = _split2(a)
    return _dot(a_hi, w_hi) + _dot(a_lo, w_hi) + _dot(a_hi, w_lo)


def _head_block_ones():
    r = lax.broadcasted_iota(jnp.int32, (LANES, LANES), 0) // HEAD
    c = lax.broadcasted_iota(jnp.int32, (LANES, LANES), 1) // HEAD
    return jnp.where(r == c, 1.0, 0.0).astype(BF16)


def _group_sum(x, ones):
    def tile_sum(t):
        hi, lo = _split2(t)
        return _dot(hi, ones) + _dot(lo, ones)

    tiles = [tile_sum(x[:, j * LANES:(j + 1) * LANES]) for j in range(x.shape[1] // LANES)]
    return tiles[0] if len(tiles) == 1 else jnp.concatenate(tiles, axis=1)


def _rmsnorm_kernel(x_ref, g_ref, o_ref):
    x = x_ref[...]
    y = x * lax.rsqrt(jnp.mean(x * x, axis=-1, keepdims=True) + RMS_EPS)
    o_ref[...] = (y * g_ref[...]).astype(o_ref.dtype)


def rmsnorm_bf16(x, g, tm):
    m, d = x.shape
    return pl.pallas_call(
        _rmsnorm_kernel,
        grid=(m // tm,),
        in_specs=[pl.BlockSpec((tm, d), lambda i: (i, 0)), pl.BlockSpec((1, d), lambda i: (0, 0))],
        out_specs=pl.BlockSpec((tm, d), lambda i: (i, 0)),
        out_shape=jax.ShapeDtypeStruct((m, d), BF16),
        compiler_params=_cparams("parallel"),
    )(x, g.reshape(1, d))


def _split_kernel(w_ref, qk_ref, v_ref, rw_ref, gt_ref, *, widths):
    outs = (qk_ref, v_ref, rw_ref, gt_ref)
    start = 0
    for o_ref, width in zip(outs, widths):
        part = w_ref[:, start:start + width].astype(o_ref.dtype)
        if o_ref.shape[1] > width:
            part = jnp.concatenate([part, jnp.zeros((part.shape[0], o_ref.shape[1] - width), o_ref.dtype)], axis=1)
        o_ref[...] = part
        start += width


def split_in_proj(w_in, layer, widths, rw_pad):
    depth, k, n = w_in.shape
    tk = 256 if k % 256 == 0 else k
    out_w = (widths[0], widths[1], rw_pad, widths[3])
    return pl.pallas_call(
        functools.partial(_split_kernel, widths=widths),
        grid=(k // tk,),
        in_specs=[pl.BlockSpec((tk, n), lambda i: (layer * (k // tk) + i, 0))],
        out_specs=[pl.BlockSpec((tk, w), lambda i: (i, 0)) for w in out_w],
        out_shape=[jax.ShapeDtypeStruct((k, w), BF16) for w in out_w],
        compiler_params=_cparams("parallel"),
    )(w_in.reshape(depth * k, n))


def _matmul_kernel(a_ref, w_ref, o_ref, *, act):
    acc = _dot(a_ref[...], w_ref[...])
    if act == "sigmoid":
        acc = _sigmoid(acc)
    o_ref[...] = acc.astype(o_ref.dtype)


def matmul(a, w, out_dtype, tm, tn, act=None):
    m, k = a.shape
    n = w.shape[1]
    return pl.pallas_call(
        functools.partial(_matmul_kernel, act=act),
        grid=(n // tn, m // tm),
        in_specs=[pl.BlockSpec((tm, k), lambda j, i: (i, 0)), pl.BlockSpec((k, tn), lambda j, i: (0, j))],
        out_specs=pl.BlockSpec((tm, tn), lambda j, i: (i, j)),
        out_shape=jax.ShapeDtypeStruct((m, n), out_dtype),
        compiler_params=_cparams("parallel", "parallel"),
    )(a, w)


def _qk_prep_kernel(qk_ref, cos_ref, sin_a_ref, sin_b_ref, gq_ref, gk_ref, q_ref, k_ref):
    ones = _head_block_ones()
    cos, sin_a, sin_b = cos_ref[...], sin_a_ref[...], sin_b_ref[...]
    n_tiles = q_ref.shape[1] // LANES
    for j in range(2 * n_tiles):
        x = qk_ref[:, j * LANES:(j + 1) * LANES]
        ms = _group_sum(x * x, ones) * (1.0 / HEAD)
        is_q = j < n_tiles
        y = x * lax.rsqrt(ms + RMS_EPS) * (gq_ref[...] if is_q else gk_ref[...])
        y = y * cos + pltpu.roll(y, LANES - ROPE_DIM // 2, 1) * sin_a + pltpu.roll(y, ROPE_DIM // 2, 1) * sin_b
        if is_q:
            q_ref[:, j * LANES:(j + 1) * LANES] = (y * HEAD ** -0.5).astype(q_ref.dtype)
        else:
            jj = j - n_tiles
            k_ref[:, jj * LANES:(jj + 1) * LANES] = y.astype(k_ref.dtype)


def qk_prep(qk, cos, sin_a, sin_b, gq, gk, seq, tm):
    m, w2 = qk.shape
    w = w2 // 2
    st = seq // tm
    tab = pl.BlockSpec((tm, LANES), lambda i: (i % st, 0))
    vec = pl.BlockSpec((1, LANES), lambda i: (0, 0))
    return pl.pallas_call(
        _qk_prep_kernel,
        grid=(m // tm,),
        in_specs=[pl.BlockSpec((tm, w2), lambda i: (i, 0)), tab, tab, tab, vec, vec],
        out_specs=[pl.BlockSpec((tm, w), lambda i: (i, 0))] * 2,
        out_shape=[jax.ShapeDtypeStruct((m, w), BF16)] * 2,
        compiler_params=_cparams("parallel"),
    )(qk, cos, sin_a, sin_b, gq, gk)


KEY_CHUNK = 256
MAX_SCORE_BOUND = 40.0


def _attn_kernel(bound_ref, lam_ref, q_ref, k_ref, v_ref, g_ref, o_ref, vaug_ref, *, lam_init):
    lp = lam_ref[...]
    lam = (jnp.exp(jnp.sum(lp[0:1] * lp[1:2], axis=-1, keepdims=True))
           - jnp.exp(jnp.sum(lp[2:3] * lp[3:4], axis=-1, keepdims=True)) + lam_init)
    seq = k_ref.shape[0]
    tq = q_ref.shape[0]
    tk = min(KEY_CHUNK, seq)

    @pl.when(pl.program_id(2) == 0)
    def _():
        vaug_ref[:, :LANES] = v_ref[...]
        vaug_ref[:, LANES:] = jnp.ones((seq, LANES), BF16)

    q = q_ref[...]
    lo = lax.broadcasted_iota(jnp.int32, (1, LANES), 1) < HEAD
    zero = jnp.zeros_like(q)
    q_lo, q_hi = jnp.where(lo, q, zero), jnp.where(lo, zero, q)

    def finish(o2):
        o = o2[:tq] - lam * o2[tq:]
        o = o * lax.rsqrt(jnp.mean(o * o, axis=-1, keepdims=True) + RMS_EPS)
        o_ref[...] = (o * g_ref[...] * (1.0 - lam_init)).astype(o_ref.dtype)

    bound = bound_ref[0, 0]
    safe = bound <= MAX_SCORE_BOUND

    @pl.when(safe)
    def _():
        q2 = jnp.concatenate([q_lo, q_hi], axis=0)
        acc = jnp.zeros((2 * tq, 2 * LANES), F32)
        for c in range(seq // tk):
            s = _dot_nt(q2, k_ref[c * tk:(c + 1) * tk, :])
            acc = acc + _dot(jnp.exp(s - bound).astype(BF16), vaug_ref[c * tk:(c + 1) * tk, :])
        finish(acc[:, :LANES] / acc[:, LANES:LANES + 1])

    @pl.when(jnp.logical_not(safe))
    def _():
        def branch(qm):
            s = _dot_nt(qm, k_ref[...])
            p = jnp.exp(s - jnp.max(s, axis=-1, keepdims=True))
            return _dot(p.astype(BF16), v_ref[...]) / jnp.sum(p, axis=-1, keepdims=True)

        finish(jnp.concatenate([branch(q_lo), branch(q_hi)], axis=0))


def diff_attention(q, k, v, score_bound, lam_params, subln_g, batch, lam_init, tq):
    m, w = q.shape
    seq = m // batch
    nq = seq // tq
    return pl.pallas_call(
        functools.partial(_attn_kernel, lam_init=lam_init),
        grid=(batch, w // LANES, nq),
        in_specs=[pl.BlockSpec(memory_space=pltpu.SMEM),
                  pl.BlockSpec((4, HEAD), lambda b, h, i: (0, 0)),
                  pl.BlockSpec((tq, LANES), lambda b, h, i: (b * nq + i, h)),
                  pl.BlockSpec((seq, LANES), lambda b, h, i: (b, h)),
                  pl.BlockSpec((seq, LANES), lambda b, h, i: (b, h)),
                  pl.BlockSpec((1, LANES), lambda b, h, i: (0, 0))],
        out_specs=pl.BlockSpec((tq, LANES), lambda b, h, i: (b * nq + i, h)),
        out_shape=jax.ShapeDtypeStruct((m, w), BF16),
        scratch_shapes=[pltpu.VMEM((seq, 2 * LANES), BF16)],
        compiler_params=_cparams("parallel", "parallel", "arbitrary"),
    )(score_bound, lam_params, q, k, v, subln_g.reshape(1, LANES))


def _rwkv_prep_kernel(p_ref, prev_ref, next_ref, mu_ref, w0_ref, w2h_ref, w2l_ref, a0_ref, a2_ref,
                      g2_ref, kkw_ref, ka_ref, rk_ref,
                      r_o, v_o, kk_o, kf_o, kb_o, lwf_o, lwb_o, af_o, ab_o, bonus_o, g_o, *, seq_tiles):
    tm = p_ref.shape[0]
    c = r_o.shape[1]
    ti = pl.program_id(0) % seq_tiles
    p = p_ref[...]
    row = lax.broadcasted_iota(jnp.int32, (tm, 1), 0)
    prev_row = jnp.where(ti > 0, prev_ref[7:8, :], 0.0)
    next_row = jnp.where(ti < seq_tiles - 1, next_ref[0:1, :], 0.0)
    prev = jnp.where(row == 0, prev_row, pltpu.roll(p, 1, 0))
    nxt = jnp.where(row == tm - 1, next_row, pltpu.roll(p, tm - 1, 0))
    p = p + mu_ref[...] * (0.5 * (prev + nxt) - p)

    r, k, v = p[:, :c], p[:, c:2 * c], p[:, 2 * c:3 * c]
    wd = p[:, 3 * c:3 * c + LANES]
    ad = p[:, 3 * c + LANES:3 * c + 2 * LANES]
    gd = p[:, 3 * c + 2 * LANES:]
    wl = w0_ref[...] + _dot_f32(jnp.tanh(wd), w2h_ref[...], w2l_ref[...])
    lw = -math.exp(-0.5) * _sigmoid(wl)
    a = _sigmoid(a0_ref[...] + _dot(ad.astype(BF16), a2_ref[...]))
    g_o[...] = _dot(_sigmoid(gd).astype(BF16), g2_ref[...])
    a_f, a_b = a[:, :c], a[:, c:]
    ones = _head_block_ones()
    kk = k * kkw_ref[...]
    kk = kk / jnp.maximum(jnp.sqrt(_group_sum(kk * kk, ones)), 1e-12)
    k_a = ka_ref[...]
    k_f = k * (1.0 + (a_f - 1.0) * k_a)
    k_b = k * (1.0 + (a_b - 1.0) * k_a)
    bonus_o[...] = _group_sum(r * (k_f + k_b) * rk_ref[...], ones) * v
    r_o[...] = r
    v_o[...] = v
    kk_o[...] = kk
    kf_o[...] = k_f
    kb_o[...] = k_b
    lwf_o[...] = lw[:, :c]
    lwb_o[...] = lw[:, c:]
    af_o[...] = a_f
    ab_o[...] = a_b


def rwkv_prep(p, mu, w0, w2, a0, a2, g2, k_k, k_a, r_k, seq, tm, c):
    m, pc = p.shape
    st = seq // tm
    hb = tm // 8
    last = m // 8 - 1
    full = lambda a: pl.BlockSpec(a.shape, lambda i: (0, 0))
    row_c = pl.BlockSpec((tm, c), lambda i: (i, 0))
    small = [mu, w0, w2[0], w2[1], a0, a2, g2, k_k, k_a, r_k]
    return pl.pallas_call(
        functools.partial(_rwkv_prep_kernel, seq_tiles=st),
        grid=(m // tm,),
        in_specs=[pl.BlockSpec((tm, pc), lambda i: (i, 0)),
                  pl.BlockSpec((8, pc), lambda i: (jnp.maximum(i * hb - 1, 0), 0)),
                  pl.BlockSpec((8, pc), lambda i: (jnp.minimum((i + 1) * hb, last), 0))] + [full(a) for a in small],
        out_specs=[row_c] * 11,
        out_shape=[jax.ShapeDtypeStruct((m, c), F32)] * 11,
        compiler_params=_cparams("parallel"),
    )(p, p, p, *small)


def _rwkv_post_kernel(yf_ref, yb_ref, bonus_ref, g_ref, lnw_ref, lnb_ref, o_ref):
    ones = _head_block_ones()
    y = yf_ref[...] + yb_ref[...]
    mean = _group_sum(y, ones) * (1.0 / HEAD)
    yc = y - mean
    var = _group_sum(yc * yc, ones) * (1.0 / HEAD)
    yn = yc * lax.rsqrt(var + GN_EPS) * lnw_ref[...] + lnb_ref[...]
    o_ref[...] = ((yn + bonus_ref[...]) * g_ref[...]).astype(o_ref.dtype)


def rwkv_post(yf, yb, bonus, g, ln_w, ln_b, tm):
    m, c = yf.shape
    row = pl.BlockSpec((tm, c), lambda i: (i, 0))
    vec = pl.BlockSpec((1, c), lambda i: (0, 0))
    return pl.pallas_call(
        _rwkv_post_kernel,
        grid=(m // tm,),
        in_specs=[row] * 4 + [vec] * 2,
        out_specs=row,
        out_shape=jax.ShapeDtypeStruct((m, c), BF16),
        compiler_params=_cparams("parallel"),
    )(yf, yb, bonus, g, ln_w.reshape(1, c), ln_b.reshape(1, c))


MERGE_ROWS = 128


def _merge_kernel(ya_ref, yb_ref, gate_ref, x_ref, wa_ref, wb_ref, wo_ref, fg_ref, wrh_ref, wrl_ref,
                  h_ref, hn_ref, aff_ref):
    d = x_ref.shape[1]
    tm = x_ref.shape[0]
    sub = min(MERGE_ROWS, tm)
    rows = [slice(r * sub, (r + 1) * sub) for r in range(tm // sub)]
    pa = [_dot(ya_ref[r, :], wa_ref[...]) for r in rows]
    pb = [_dot(yb_ref[r, :], wb_ref[...]) for r in rows]
    merged = [(gate_ref[r, :d].astype(F32) * a + gate_ref[r, d:].astype(F32) * b).astype(BF16)
              for r, a, b in zip(rows, pa, pb)]
    hs = [x_ref[r, :] + _dot(mg, wo_ref[...]) for r, mg in zip(rows, merged)]
    for r, h in zip(rows, hs):
        h_ref[r, :] = h
        hn = h * lax.rsqrt(jnp.mean(h * h, axis=-1, keepdims=True) + RMS_EPS) * fg_ref[...]
        hn_ref[r, :] = hn.astype(hn_ref.dtype)
        hn_hi, hn_lo = _split2(hn)
        wr_hi = wrh_ref[...]
        logits = _dot_nt(wr_hi, hn_hi) + _dot_nt(wr_hi, hn_lo) + _dot_nt(wrl_ref[...], hn_hi)
        e = jnp.exp(logits - jnp.max(logits, axis=0, keepdims=True))
        aff_ref[:, r] = e / jnp.sum(e, axis=0, keepdims=True)


def merge_out_router(ya, yb, gates, x, wa, wb, wo, ffn_g, wr_hi, wr_lo, tm):
    m, d = x.shape
    full = lambda a: pl.BlockSpec(a.shape, lambda i: (0, 0), pipeline_mode=pl.Buffered(1))
    row = lambda a: pl.BlockSpec((tm, a.shape[1]), lambda i: (i, 0))
    ne = wr_hi.shape[0]
    return pl.pallas_call(
        _merge_kernel,
        grid=(m // tm,),
        in_specs=[row(ya), row(yb), row(gates), row(x), full(wa), full(wb), full(wo), full(ffn_g),
                  full(wr_hi), full(wr_lo)],
        out_specs=[pl.BlockSpec((tm, d), lambda i: (i, 0)), pl.BlockSpec((tm, d), lambda i: (i, 0)),
                   pl.BlockSpec((ne, tm), lambda i: (0, i))],
        out_shape=[jax.ShapeDtypeStruct((m, d), F32), jax.ShapeDtypeStruct((m, d), BF16),
                   jax.ShapeDtypeStruct((ne, m), F32)],
        compiler_params=_cparams("parallel"),
    )(ya, yb, gates, x, wa, wb, wo, ffn_g, wr_hi, wr_lo)


def _prefix_excl(m_bf16, upper):
    rows, t = m_bf16.shape
    off = jnp.zeros((rows, 1), F32)
    out = []
    for j in range(t // LANES):
        blk = m_bf16[:, j * LANES:(j + 1) * LANES]
        out.append(_dot(blk, upper) + off)
        off = off + jnp.sum(blk.astype(F32), axis=-1, keepdims=True)
    return jnp.concatenate(out, axis=1)


TOKEN_BLOCK = 256
SLOT_WINDOW = 128
SLOT_ALIGN = 16


def _select_kernel(aff_ref, pos_ref, cnt_ref, *, cap):
    aff = aff_ref[...]
    count = lambda pred: jnp.sum(jnp.where(pred, 1.0, 0.0), axis=-1, keepdims=True)
    as_f32 = lambda bits: pltpu.bitcast(jnp.broadcast_to(bits, aff.shape), F32)

    def step(i, thr):
        cand = thr | (jnp.int32(1) << (30 - i))
        return jnp.where(count(aff >= as_f32(cand)) >= cap, cand, thr)

    thr = lax.fori_loop(0, 31, step, jnp.zeros((aff.shape[0], 1), jnp.int32))
    gt = aff >= as_f32(thr + 1)
    eq = (aff >= as_f32(thr)) & jnp.logical_not(gt)
    r = lax.broadcasted_iota(jnp.int32, (LANES, LANES), 0)
    c = lax.broadcasted_iota(jnp.int32, (LANES, LANES), 1)
    upper = jnp.where(r < c, 1.0, 0.0).astype(BF16)
    need = cap - jnp.sum(jnp.where(gt, 1.0, 0.0), axis=-1, keepdims=True)
    eq_rank = _prefix_excl(jnp.where(eq, 1.0, 0.0).astype(BF16), upper)
    sel = gt | (eq & (eq_rank < need))
    sel_bf = jnp.where(sel, 1.0, 0.0).astype(BF16)
    pos = _prefix_excl(sel_bf, upper)
    pos_ref[...] = jnp.where(sel, pos, -1.0).astype(jnp.int32)
    seq = aff.shape[1]
    tok = lax.broadcasted_iota(jnp.int32, (seq, LANES), 0)
    blk = lax.broadcasted_iota(jnp.int32, (seq, LANES), 1)
    before = jnp.where(tok < blk * TOKEN_BLOCK, 1.0, 0.0).astype(BF16)
    cnt_ref[0] = _dot(sel_bf, before).astype(jnp.int32)


def select_slots(aff_t, batch, cap):
    ne, m = aff_t.shape
    seq = m // batch
    assert seq // TOKEN_BLOCK < LANES
    return pl.pallas_call(
        functools.partial(_select_kernel, cap=cap),
        grid=(batch,),
        in_specs=[pl.BlockSpec((ne, seq), lambda b: (0, b))],
        out_specs=[pl.BlockSpec((ne, seq), lambda b: (0, b)), pl.BlockSpec((1, ne, LANES), lambda b: (b, 0, 0))],
        out_shape=[jax.ShapeDtypeStruct((ne, m), jnp.int32), jax.ShapeDtypeStruct((batch, ne, LANES), jnp.int32)],
        compiler_params=_cparams("parallel"),
    )(aff_t)


def _for_each_window(cnt_ref, bases, n_blocks, sw, cap, step):
    def start(x, j):
        return (cnt_ref[bases[x] + j] // SLOT_ALIGN) * SLOT_ALIGN

    def window(x, lo):
        s0 = pl.multiple_of(jnp.minimum(lo, cap - sw), SLOT_ALIGN)
        slot = s0 + lax.broadcasted_iota(jnp.int32, (sw, 1), 0)
        return x, s0, lambda pos_blk: (pos_blk == slot) & (slot >= lo)

    for j in range(n_blocks):
        step(j, [window(x, start(x, j)) for x in range(len(bases))])
    for j in range(n_blocks):
        for x in range(len(bases)):
            lo0 = start(x, j)
            n_win = (cnt_ref[bases[x] + j + 1] - lo0 + sw - 1) // sw

            def extra(k, carry, j=j, x=x, lo0=lo0):
                step(j, [window(x, lo0 + k * sw)])
                return carry

            lax.fori_loop(1, n_win, extra, 0)


GATHER_GROUP = 1
SCATTER_GROUP = 2


def _group_bases(batch_id, group_id, n_groups, group):
    first = (batch_id * n_groups + group_id) * group
    return [(first + x) * LANES for x in range(group)]


def _gather_kernel(cnt_ref, pos_ref, aff_ref, hn_ref, xe_ref, gate_ref, acc_ref, gacc_ref):
    seq = hn_ref.shape[0]
    group, cap, _ = acc_ref.shape
    tb = min(TOKEN_BLOCK, seq)
    sw = min(SLOT_WINDOW, cap)
    bases = _group_bases(pl.program_id(0), pl.program_id(1), pl.num_programs(1), group)
    acc_ref[...] = jnp.zeros_like(acc_ref)
    gacc_ref[...] = jnp.zeros_like(gacc_ref)

    def step(j, windows):
        tok = slice(j * tb, (j + 1) * tb)
        hits = [hit_of(pos_ref[x, :, tok]) for x, _, hit_of in windows]
        onehot = jnp.concatenate([jnp.where(h, 1.0, 0.0).astype(BF16) for h in hits], axis=0)
        rows = _dot(onehot, hn_ref[tok, :])
        for i, ((x, s0, _), h) in enumerate(zip(windows, hits)):
            acc_ref[x, pl.ds(s0, sw), :] += rows[i * sw:(i + 1) * sw]
            gacc_ref[x, pl.ds(s0, sw), :] += jnp.sum(jnp.where(h, aff_ref[x, :, tok], 0.0), axis=-1, keepdims=True)

    _for_each_window(cnt_ref, bases, seq // tb, sw, cap, step)
    xe_ref[:, 0] = acc_ref[...].astype(xe_ref.dtype)
    gate_ref[:, 0] = gacc_ref[...]


def moe_gather(cnt, pos3, aff3, hn, batch, cap):
    ne = pos3.shape[0]
    m, d = hn.shape
    seq = m // batch
    g = GATHER_GROUP
    assert cap % min(SLOT_WINDOW, cap) == 0 and ne % g == 0
    row = pl.BlockSpec((g, 1, seq), lambda b, e, c: (e, 0, b))
    return pl.pallas_call(
        _gather_kernel,
        grid_spec=pltpu.PrefetchScalarGridSpec(
            num_scalar_prefetch=1,
            grid=(batch, ne // g),
            in_specs=[row, row, pl.BlockSpec((seq, d), lambda b, e, c: (b, 0))],
            out_specs=[pl.BlockSpec((g, 1, cap, d), lambda b, e, c: (e, b, 0, 0)),
                       pl.BlockSpec((g, 1, cap, 1), lambda b, e, c: (e, b, 0, 0))],
            scratch_shapes=[pltpu.VMEM((g, cap, d), F32), pltpu.VMEM((g, cap, 1), F32)]),
        out_shape=[jax.ShapeDtypeStruct((ne, batch, cap, d), BF16),
                   jax.ShapeDtypeStruct((ne, batch, cap, 1), F32)],
        compiler_params=_cparams("parallel", "parallel"),
    )(cnt, pos3, aff3, hn)


def _expert_kernel(xe_ref, gate_ref, wg_ref, wu_ref, hid_ref, wg_bf, wu_bf):
    @pl.when(pl.program_id(1) == 0)
    def _():
        wg_bf[...] = wg_ref[0].astype(BF16)
        wu_bf[...] = wu_ref[0].astype(BF16)

    xe = xe_ref[0, 0]
    hg = _dot(xe, wg_bf[...])
    hu = _dot(xe, wu_bf[...])
    hid_ref[0, 0] = (hg * _sigmoid(hg) * hu * gate_ref[0, 0]).astype(hid_ref.dtype)


def moe_experts(xe, gate, wg, wu):
    ne, batch, cap, d = xe.shape
    ff = wg.shape[2]
    return pl.pallas_call(
        _expert_kernel,
        grid=(ne, batch),
        in_specs=[pl.BlockSpec((1, 1, cap, d), lambda e, b: (e, b, 0, 0)),
                  pl.BlockSpec((1, 1, cap, 1), lambda e, b: (e, b, 0, 0)),
                  pl.BlockSpec((1, d, ff), lambda e, b: (e, 0, 0)),
                  pl.BlockSpec((1, d, ff), lambda e, b: (e, 0, 0))],
        out_specs=pl.BlockSpec((1, 1, cap, ff), lambda e, b: (e, b, 0, 0)),
        out_shape=jax.ShapeDtypeStruct((ne, batch, cap, ff), BF16),
        scratch_shapes=[pltpu.VMEM((d, ff), BF16)] * 2,
        compiler_params=_cparams("parallel", "arbitrary"),
    )(xe, gate, wg, wu)


def _down_kernel(hid_ref, wd_ref, ye_ref, wd_bf):
    @pl.when(pl.program_id(1) == 0)
    def _():
        wd_bf[...] = wd_ref[0].astype(BF16)

    ye_ref[0, 0] = _dot(hid_ref[0, 0], wd_bf[...]).astype(ye_ref.dtype)


def moe_down(hid, wd):
    ne, batch, cap, ff = hid.shape
    d = wd.shape[2]
    return pl.pallas_call(
        _down_kernel,
        grid=(ne, batch),
        in_specs=[pl.BlockSpec((1, 1, cap, ff), lambda e, b: (e, b, 0, 0)),
                  pl.BlockSpec((1, ff, d), lambda e, b: (e, 0, 0))],
        out_specs=pl.BlockSpec((1, 1, cap, d), lambda e, b: (e, b, 0, 0)),
        out_shape=jax.ShapeDtypeStruct((ne, batch, cap, d), BF16),
        scratch_shapes=[pltpu.VMEM((ff, d), BF16)],
        compiler_params=_cparams("parallel", "arbitrary"),
    )(hid, wd)


def _scatter_kernel(cnt_ref, pos_ref, ye_ref, h_ref, o_ref):
    e = pl.program_id(2)
    tt = h_ref.shape[0]
    group, _, cap, _ = ye_ref.shape
    tb = min(TOKEN_BLOCK, tt)
    sw = min(SLOT_WINDOW, cap)
    tile_off = pl.program_id(1) * (tt // tb)
    bases = [b + tile_off for b in _group_bases(pl.program_id(0), e, pl.num_programs(2), group)]

    @pl.when(e == 0)
    def _():
        o_ref[...] = h_ref[...]

    def step(j, windows):
        tok = slice(j * tb, (j + 1) * tb)
        onehot = jnp.concatenate([jnp.where(hit_of(pos_ref[x, :, tok]), 1.0, 0.0).astype(BF16)
                                  for x, _, hit_of in windows], axis=0)
        ye = jnp.concatenate([ye_ref[x, 0, pl.ds(s0, sw), :] for x, s0, _ in windows], axis=0)
        o_ref[tok, :] += _dot_tn(onehot, ye)

    _for_each_window(cnt_ref, bases, tt // tb, sw, cap, step)


def moe_scatter(cnt, pos3, ye, h, batch, tt):
    ne, _, cap, d = ye.shape
    m = h.shape[0]
    nt = m // batch // tt
    g = SCATTER_GROUP
    assert ne % g == 0
    return pl.pallas_call(
        _scatter_kernel,
        grid_spec=pltpu.PrefetchScalarGridSpec(
            num_scalar_prefetch=1,
            grid=(batch, nt, ne // g),
            in_specs=[pl.BlockSpec((g, 1, tt), lambda b, i, e, c: (e, 0, b * nt + i)),
                      pl.BlockSpec((g, 1, cap, d), lambda b, i, e, c: (e, b, 0, 0)),
                      pl.BlockSpec((tt, d), lambda b, i, e, c: (b * nt + i, 0))],
            out_specs=pl.BlockSpec((tt, d), lambda b, i, e, c: (b * nt + i, 0))),
        out_shape=jax.ShapeDtypeStruct((m, d), F32),
        compiler_params=_cparams("parallel", "parallel", "arbitrary"),
    )(cnt, pos3, ye, h)


def _rope_tables(seq):
    half = ROPE_DIM // 2
    inv = ROPE_THETA ** (-(jnp.arange(0, ROPE_DIM, 2, dtype=F32) / ROPE_DIM))
    ang = jnp.arange(seq, dtype=F32)[:, None] * inv[None, :]
    cos, sin = jnp.cos(ang), jnp.sin(ang)
    one = jnp.ones((seq, HEAD - ROPE_DIM), F32)
    zero = lambda n: jnp.zeros((seq, n), F32)
    cos_t = jnp.concatenate([cos, cos, one], axis=1)
    sin_a = jnp.concatenate([-sin, zero(HEAD - half)], axis=1)
    sin_b = jnp.concatenate([zero(half), sin, zero(HEAD - ROPE_DIM)], axis=1)
    return [jnp.tile(t, (1, LANES // HEAD)) for t in (cos_t, sin_a, sin_b)]


def _hi_lo(w):
    hi = w.astype(BF16)
    return hi, (w - hi.astype(F32)).astype(BF16)


def _block_diag2(wf, wb):
    z = jnp.zeros_like(wf)
    return jnp.concatenate([jnp.concatenate([wf, z], axis=1), jnp.concatenate([z, wb], axis=1)], axis=0)


def _col_tile(n):
    for t in (1024, 896, 512, 256, 128):
        if n % t == 0:
            return t
    raise ValueError(f"unsupported matmul width {n}")


def kernel(x, attn_norm_g, w_in, q_norm_g, k_norm_g, lambda_q1, lambda_k1, lambda_q2, lambda_k2, subln_g, shift_mu, w0_f, w2_f, w0_b, w2_b, a0_f, a2_f, a0_b, a2_b, g2, k_k, k_a, r_k, ln_x_w, ln_x_b, w_branch_a, w_branch_b, w_out, ffn_norm_g, w_router, w_gate_e, w_up_e, w_down_e):
    batch, seq, d = x.shape
    m = batch * seq
    depth = w_in.shape[0]
    c = w_branch_b.shape[1]
    qk_w = 2 * w_branch_a.shape[1]
    v_w = w_branch_a.shape[1]
    rw_cols = shift_mu.shape[1]
    rw_pad = -(-rw_cols // LANES) * LANES
    ne = w_router.shape[2]
    cap = 2 * seq // ne
    tm = min(256, seq)
    tmm = next((t for t in (1024, 512) if m % t == 0), tm)
    cos_t, sin_a, sin_b = _rope_tables(seq)
    row = lambda a: a.reshape(1, -1)

    h = x.reshape(m, d)
    for l in range(depth):
        lam_init = 0.8 - 0.6 * math.exp(-0.3 * l)
        w_qk, w_v, w_rw, w_gt = split_in_proj(w_in, l, (qk_w, v_w, rw_cols, 2 * d), rw_pad)

        hn = rmsnorm_bf16(h, attn_norm_g[l], tmm)
        qk = matmul(hn, w_qk, F32, tmm, _col_tile(qk_w))
        v = matmul(hn, w_v, BF16, tmm, _col_tile(v_w))
        p_rw = matmul(hn, w_rw, F32, tmm, _col_tile(rw_pad))
        gates = matmul(hn, w_gt, BF16, tmm, _col_tile(2 * d), act="sigmoid")

        gq = row(jnp.tile(q_norm_g[l], LANES // HEAD))
        gk = row(jnp.tile(k_norm_g[l], LANES // HEAD))
        q, k = qk_prep(qk, cos_t, sin_a, sin_b, gq, gk, seq, tm)
        lam_params = jnp.stack([lambda_q1[l], lambda_k1[l], lambda_q2[l], lambda_k2[l]])
        score_bound = (1.02 * math.sqrt(HEAD) * jnp.max(jnp.abs(q_norm_g[l])) * jnp.max(jnp.abs(k_norm_g[l]))
                       ).astype(F32).reshape(1, 1)
        y_a = diff_attention(q, k, v, score_bound, lam_params, subln_g[l], batch, lam_init, min(512, seq))

        mu = jnp.pad(shift_mu[l], (0, rw_pad - rw_cols)).reshape(1, rw_pad)
        w0 = row(jnp.concatenate([w0_f[l], w0_b[l]]))
        a0 = row(jnp.concatenate([a0_f[l], a0_b[l]]))
        w2 = _hi_lo(_block_diag2(w2_f[l], w2_b[l]))
        a2 = _block_diag2(a2_f[l], a2_b[l]).astype(BF16)
        g_rows = rw_pad - 3 * c - 2 * LANES
        g2p = jnp.pad(g2[l], ((0, g_rows - g2.shape[1]), (0, 0))).astype(BF16)
        r, vv, kk, k_f, k_b, lw_f, lw_b, a_f, a_b, bonus, g = rwkv_prep(
            p_rw, mu, w0, w2, a0, a2, g2p, row(k_k[l]), row(k_a[l]), row(r_k[l]), seq, tm, c)
        y_f, y_bk = rwkv_scan(r, vv, kk, lw_f, k_f, a_f, lw_b, k_b, a_b, batch)
        y_b = rwkv_post(y_f, y_bk, bonus, g, ln_x_w[l], ln_x_b[l], tm)

        wr_hi, wr_lo = _hi_lo(w_router[l].T)
        h2, hn2, aff_t = merge_out_router(
            y_a, y_b, gates, h, w_branch_a[l].astype(BF16), w_branch_b[l].astype(BF16), w_out[l].astype(BF16),
            row(ffn_norm_g[l]), wr_hi, wr_lo, tm)

        pos, cnt = select_slots(aff_t, batch, cap)
        pos3 = pos.reshape(ne, 1, m)
        cnt = cnt.reshape(-1)
        xe, gate = moe_gather(cnt, pos3, aff_t.reshape(ne, 1, m), hn2, batch, cap)
        hid = moe_experts(xe, gate, w_gate_e[l], w_up_e[l])
        ye = moe_down(hid, w_down_e[l])
        h = moe_scatter(cnt, pos3, ye, h2, batch, min(1024, seq))
    return h.reshape(batch, seq, d)
```
